```python
import jax, jax.numpy as jnp
from jax import lax
import numpy as np

D_MODEL = 1024
BATCH = 2
SEQ = 8192
DEPTH = 1
DEC_BATCH = 16
DEC_SEQ = 4096
PAST_LEN = 128

GRID_W = 64
N_MEM = 256
EPS = 1e-6
POOL_GROUPS = 4
POOL_GROUP_DIM = D_MODEL // 8
POOL_WIDTH = POOL_GROUPS * POOL_GROUP_DIM
POOL_WINDOWS = (2, 4, 8, 16)
N_HEADS = 8
N_KV_HEADS = 2
HEAD_DIM = 64
ATTN_WIDTH = N_HEADS * HEAD_DIM
KV_WIDTH = N_KV_HEADS * HEAD_DIM
AXIS_DIM = HEAD_DIM // 2
ROPE_THETA = 10000.0
Q_BLOCK = 128
N_X_HEADS = 4
X_HEAD_DIM = D_MODEL // 8
X_WIDTH = N_X_HEADS * X_HEAD_DIM
N_BRANCH = 3
BRANCH_WIDTH = 512
IN_WIDTHS = (POOL_WIDTH, POOL_WIDTH, ATTN_WIDTH, KV_WIDTH, KV_WIDTH, ATTN_WIDTH, X_WIDTH, X_WIDTH, N_BRANCH * D_MODEL)
IN_DIM = 2 * POOL_WIDTH + 2 * ATTN_WIDTH + 2 * KV_WIDTH + 2 * X_WIDTH + N_BRANCH * D_MODEL

kernel_name = "hybrid_pool_gqa_xattn_gated_encoder"


def rms_norm(x, g):
    xf = x.astype(jnp.float32)
    y = xf * lax.rsqrt(jnp.mean(xf * xf, axis=-1, keepdims=True) + EPS)
    return (y * g.astype(jnp.float32)).astype(x.dtype)


def axial_rope_tables(L):
    rows = L // GRID_W
    row = jnp.repeat(jnp.arange(rows, dtype=jnp.float32), GRID_W)
    col = jnp.tile(jnp.arange(GRID_W, dtype=jnp.float32), rows)
    inv = ROPE_THETA ** (-jnp.arange(0, AXIS_DIM, 2, dtype=jnp.float32) / AXIS_DIM)
    ang = jnp.concatenate([row[:, None] * inv, col[:, None] * inv], axis=-1)
    return jnp.cos(ang), jnp.sin(ang)


def apply_rope(x, cos, sin):
    B, L, H, D = x.shape
    xf = x.astype(jnp.float32).reshape(B, L, H, D // 2, 2)
    x0, x1 = xf[..., 0], xf[..., 1]
    c = cos[None, :, None, :]
    s = sin[None, :, None, :]
    out = jnp.stack([x0 * c - x1 * s, x0 * s + x1 * c], axis=-1)
    return out.reshape(B, L, H, D).astype(x.dtype)


def multiscale_pool(u, w_pool, pool_scale):
    B, L, _ = u.shape
    uf = u.astype(jnp.float32).reshape(B, L, POOL_GROUPS, POOL_GROUP_DIM)
    cs = jnp.concatenate([jnp.zeros((B, 1, POOL_GROUPS, POOL_GROUP_DIM), jnp.float32),
                          jnp.cumsum(uf, axis=1)], axis=1)
    t = jnp.arange(L, dtype=jnp.int32)
    pooled = []
    for g, w in enumerate(POOL_WINDOWS):
        lo = jnp.clip(t - w // 2, 0, L)
        hi = jnp.clip(t + (w - 1 - w // 2) + 1, 0, L)
        csg = cs[:, :, g]
        ssum = jnp.take(csg, hi, axis=1) - jnp.take(csg, lo, axis=1)
        cnt = (hi - lo).astype(jnp.float32)
        pooled.append(ssum / cnt[None, :, None])
    pooled = jnp.stack(pooled, axis=2)
    mixed = (pooled - uf).astype(u.dtype)
    out = jnp.einsum('blgc,gcd->blgd', mixed, w_pool).reshape(B, L, POOL_WIDTH)
    return out * pool_scale


def self_attention(q, k, v):
    B, L = q.shape[:2]
    G = N_HEADS // N_KV_HEADS
    nb = L // Q_BLOCK
    qb = q.reshape(B, nb, Q_BLOCK, N_KV_HEADS, G, HEAD_DIM).transpose(1, 0, 2, 3, 4, 5)
    scale = HEAD_DIM ** -0.5

    def one_block(qblk):
        s = jnp.einsum('bqkgd,bskd->bkgqs', qblk, k).astype(jnp.float32) * scale
        p = jax.nn.softmax(s, axis=-1).astype(v.dtype)
        return jnp.einsum('bkgqs,bskd->bqkgd', p, v)

    o = lax.map(one_block, qb)
    return o.transpose(1, 0, 2, 3, 4, 5).reshape(B, L, ATTN_WIDTH)


def cross_attention(xq, mem_n, w_mem_kv):
    B, L, _ = xq.shape
    M = mem_n.shape[1]
    kv = mem_n @ w_mem_kv
    mk = kv[..., :X_WIDTH].reshape(B, M, N_X_HEADS, X_HEAD_DIM)
    mv = kv[..., X_WIDTH:].reshape(B, M, N_X_HEADS, X_HEAD_DIM)
    q = xq.reshape(B, L, N_X_HEADS, X_HEAD_DIM)
    s = jnp.einsum('blhd,bmhd->bhlm', q, mk).astype(jnp.float32) * (X_HEAD_DIM ** -0.5)
    p = jax.nn.softmax(s, axis=-1).astype(mv.dtype)
    return jnp.einsum('bhlm,bmhd->blhd', p, mv).reshape(B, L, X_WIDTH)


def encoder_layer(x, mem, ln_pre, ln_post, ln_mem, w_in, b_merge, q_norm, k_norm,
                  w_pool, pool_scale, w_mem_kv, w_branch, w_out):
    B, L, _ = x.shape
    h = rms_norm(x, ln_pre)
    z = h @ w_in
    split_at = np.cumsum(IN_WIDTHS)[:-1].tolist()
    (pool_in, pool_gate, q, k, v, attn_gate, xq, x_gate, merge_logits) = jnp.split(z, split_at, axis=-1)

    pool_out = multiscale_pool(pool_in, w_pool, pool_scale)

    cos, sin = axial_rope_tables(L)
    q = apply_rope(rms_norm(q.reshape(B, L, N_HEADS, HEAD_DIM), q_norm), cos, sin)
    k = apply_rope(rms_norm(k.reshape(B, L, N_KV_HEADS, HEAD_DIM), k_norm), cos, sin)
    attn_out = self_attention(q, k, v.reshape(B, L, N_KV_HEADS, HEAD_DIM))

    cross_out = cross_attention(xq, rms_norm(mem, ln_mem), w_mem_kv)

    branches = (pool_out * jax.nn.silu(pool_gate),
                attn_out * jax.nn.silu(attn_gate),
                cross_out * jax.nn.silu(x_gate))
    gates = jax.nn.sigmoid(merge_logits.reshape(B, L, N_BRANCH, D_MODEL) + b_merge)
    merged = jnp.zeros_like(x)
    for n in range(N_BRANCH):
        merged = merged + gates[:, :, n] * (branches[n] @ w_branch[n])

    y = rms_norm(merged @ w_out, ln_post)
    return x + y


def setup_inputs(seed: int = 0) -> dict:
    key = jax.random.key(seed)
    ks = jax.random.split(key, 20)
    f32 = jnp.float32
    nrm = lambda k, shape, s: (jax.random.normal(k, shape, f32) * s).astype(f32)
    return {
        "x_prompt": nrm(ks[0], (BATCH, SEQ, D_MODEL), 1.0),
        "x_sample": nrm(ks[1], (DEC_BATCH, DEC_SEQ, D_MODEL), 1.0),
        "mem_prompt": nrm(ks[2], (BATCH, N_MEM, D_MODEL), 1.0),
        "mem_sample": nrm(ks[3], (DEC_BATCH, N_MEM, D_MODEL), 1.0),
        "ln_pre": 1.0 + nrm(ks[4], (D_MODEL,), 0.02),
        "ln_post": 1.0 + nrm(ks[5], (D_MODEL,), 0.02),
        "ln_mem": 1.0 + nrm(ks[6], (D_MODEL,), 0.02),
        "w_in": nrm(ks[7], (D_MODEL, IN_DIM), D_MODEL ** -0.5),
        "b_merge": nrm(ks[8], (N_BRANCH, D_MODEL), 0.01),
        "q_norm": 1.0 + nrm(ks[9], (HEAD_DIM,), 0.02),
        "k_norm": 1.0 + nrm(ks[10], (HEAD_DIM,), 0.02),
        "w_pool": nrm(ks[11], (POOL_GROUPS, POOL_GROUP_DIM, POOL_GROUP_DIM), POOL_GROUP_DIM ** -0.5),
        "pool_scale": 1.0 + nrm(ks[12], (POOL_WIDTH,), 0.1),
        "w_mem_kv": nrm(ks[13], (D_MODEL, 2 * X_WIDTH), D_MODEL ** -0.5),
        "w_branch": nrm(ks[14], (N_BRANCH, BRANCH_WIDTH, D_MODEL), BRANCH_WIDTH ** -0.5),
        "w_out": nrm(ks[15], (D_MODEL, D_MODEL), D_MODEL ** -0.5),
    }


def reference(x_prompt, x_sample, mem_prompt, mem_sample, ln_pre, ln_post, ln_mem, w_in, b_merge,
              q_norm, k_norm, w_pool, pool_scale, w_mem_kv, w_branch, w_out):
    y_prompt = x_prompt
    y_sample = x_sample
    for _ in range(DEPTH):
        y_prompt = encoder_layer(y_prompt, mem_prompt, ln_pre, ln_post, ln_mem, w_in, b_merge, q_norm, k_norm,
                                 w_pool, pool_scale, w_mem_kv, w_branch, w_out)
        y_sample = encoder_layer(y_sample, mem_sample, ln_pre, ln_post, ln_mem, w_in, b_merge, q_norm, k_norm,
                                 w_pool, pool_scale, w_mem_kv, w_branch, w_out)
    return (y_prompt, y_sample)
```

```python
import functools

import numpy as np
import jax
import jax.numpy as jnp
from jax import lax
from jax.experimental import pallas as pl
from jax.experimental.pallas import tpu as pltpu

F32 = jnp.float32
BF16 = jnp.bfloat16

D_MODEL = 1024
GRID_W = 64
EPS = 1e-6
POOL_GROUPS = 4
POOL_GROUP_DIM = 128
POOL_WIDTH = 512
POOL_WINDOWS = (2, 4, 8, 16)
N_HEADS = 8
N_KV_HEADS = 2
HEAD_DIM = 64
HEADS_PER_KV = N_HEADS // N_KV_HEADS
ATTN_WIDTH = 512
KV_WIDTH = 128
ROPE_THETA = 10000.0
N_X_HEADS = 4
X_HEAD_DIM = 128
X_WIDTH = 512
N_BRANCH = 3
BRANCH_WIDTH = 512

_Q0 = 2 * POOL_WIDTH
_K0 = _Q0 + ATTN_WIDTH
_V0 = _K0 + KV_WIDTH
_AG0 = _V0 + KV_WIDTH
QKV_WIDTH = ATTN_WIDTH + 2 * KV_WIDTH
R_POOL_IN = 0
R_POOL_GATE = 512
R_ATTN_GATE = 1024
R_XQ = 1536
R_X_GATE = 2048
R_MERGE = 2560
REST_WIDTH = R_MERGE + N_BRANCH * D_MODEL

V_ROWS = HEAD_DIM + 16
HALO = 16
VMEM_LIMIT = 56 * 1024 * 1024


def _const_spec(shape):
    n = len(shape)
    return pl.BlockSpec(shape, lambda *_: (0,) * n, pipeline_mode=pl.Buffered(1))


def _rope(t, c, s):
    lane = lax.broadcasted_iota(jnp.int32, t.shape, 1)
    nxt = pltpu.roll(t, t.shape[1] - 1, axis=1)
    prv = pltpu.roll(t, 1, axis=1)
    return t * c + jnp.where(lane % 2 == 0, nxt, prv) * s


def _proj_kernel(x_ref, lnpre_ref, wqkv_ref, wrest_ref, bd_ref, qn_ref, kn_ref, cos_ref, sin_ref,
                 qT_ref, k_ref, vT_ref, zr_ref):
    x = x_ref[0]
    ms = jnp.mean(x * x, axis=-1, keepdims=True)
    h = (x * lax.rsqrt(ms + EPS) * lnpre_ref[...]).astype(BF16)

    for c0 in range(0, REST_WIDTH, 512):
        zr_ref[0, :, c0:c0 + 512] = jnp.dot(
            h, wrest_ref[:, c0:c0 + 512], preferred_element_type=F32).astype(BF16)

    qkv = jnp.dot(h, wqkv_ref[...], preferred_element_type=F32)
    q = qkv[:, :ATTN_WIDTH]
    k = qkv[:, ATTN_WIDTH:ATTN_WIDTH + KV_WIDTH]
    v = qkv[:, ATTN_WIDTH + KV_WIDTH:]

    bd = bd_ref[...]
    q_ms = jnp.dot((q * q).astype(BF16), bd, preferred_element_type=F32)
    k_ms = jnp.dot((k * k).astype(BF16), bd[:KV_WIDTH, :KV_WIDTH], preferred_element_type=F32)
    qn = q * lax.rsqrt(q_ms + EPS) * qn_ref[...]
    kn = k * lax.rsqrt(k_ms + EPS) * kn_ref[...]

    c = cos_ref[...]
    s = sin_ref[...]
    scale = HEAD_DIM ** -0.5
    for j in range(ATTN_WIDTH // 128):
        qt = (_rope(qn[:, 128 * j:128 * (j + 1)], c, s) * scale).T
        qT_ref[0, 2 * j] = qt[:HEAD_DIM].astype(BF16)
        qT_ref[0, 2 * j + 1] = qt[HEAD_DIM:].astype(BF16)
    kr = _rope(kn, c, s).astype(BF16)
    k_ref[0, 0] = kr[:, :HEAD_DIM]
    k_ref[0, 1] = kr[:, HEAD_DIM:]
    vt = v.T.astype(BF16)
    ones = jnp.ones((V_ROWS - HEAD_DIM, vt.shape[1]), BF16)
    for g in range(N_KV_HEADS):
        vT_ref[0, g, 0, :HEAD_DIM] = vt[g * HEAD_DIM:(g + 1) * HEAD_DIM]
        vT_ref[0, g, 0, HEAD_DIM:] = ones


def _proj_call(x, ln_pre, w_qkv, w_rest, bd, qn, kn, cos_t, sin_t, tm):
    B, L, _ = x.shape
    nblk = L // tm
    return pl.pallas_call(
        _proj_kernel,
        grid=(B, nblk),
        in_specs=[
            pl.BlockSpec((1, tm, D_MODEL), lambda b, i: (b, i, 0)),
            _const_spec((1, D_MODEL)),
            _const_spec((D_MODEL, QKV_WIDTH)),
            _const_spec((D_MODEL, REST_WIDTH)),
            _const_spec((ATTN_WIDTH, ATTN_WIDTH)),
            _const_spec((1, ATTN_WIDTH)),
            _const_spec((1, KV_WIDTH)),
            pl.BlockSpec((tm, 128), lambda b, i: (i, 0)),
            pl.BlockSpec((tm, 128), lambda b, i: (i, 0)),
        ],
        out_specs=[
            pl.BlockSpec((1, N_HEADS, HEAD_DIM, tm), lambda b, i: (b, 0, 0, i)),
            pl.BlockSpec((1, N_KV_HEADS, tm, HEAD_DIM), lambda b, i: (b, 0, i, 0)),
            pl.BlockSpec((1, N_KV_HEADS, 1, V_ROWS, tm), lambda b, i: (b, 0, i, 0, 0)),
            pl.BlockSpec((1, tm, REST_WIDTH), lambda b, i: (b, i, 0)),
        ],
        out_shape=[
            jax.ShapeDtypeStruct((B, N_HEADS, HEAD_DIM, L), BF16),
            jax.ShapeDtypeStruct((B, N_KV_HEADS, L, HEAD_DIM), BF16),
            jax.ShapeDtypeStruct((B, N_KV_HEADS, nblk, V_ROWS, tm), BF16),
            jax.ShapeDtypeStruct((B, L, REST_WIDTH), BF16),
        ],
        compiler_params=pltpu.CompilerParams(
            dimension_semantics=("arbitrary", "arbitrary"), vmem_limit_bytes=VMEM_LIMIT),
        name="proj",
    )(x, ln_pre, w_qkv, w_rest, bd, qn, kn, cos_t, sin_t)


def _mem_kernel(mem_ref, lnmem_ref, wkv_ref, mkT_ref, mv_ref):
    m = mem_ref[0]
    ms = jnp.mean(m * m, axis=-1, keepdims=True)
    mn = (m * lax.rsqrt(ms + EPS) * lnmem_ref[...]).astype(BF16)
    kv = jnp.dot(mn, wkv_ref[...], preferred_element_type=F32)
    mkT_ref[0] = kv[:, :X_WIDTH].T.astype(BF16)
    mv_ref[0] = kv[:, X_WIDTH:].astype(BF16)


def _mem_call(mem, ln_mem, w_mem_kv):
    B, M, _ = mem.shape
    return pl.pallas_call(
        _mem_kernel,
        grid=(B,),
        in_specs=[
            pl.BlockSpec((1, M, D_MODEL), lambda b: (b, 0, 0)),
            _const_spec((1, D_MODEL)),
            _const_spec((D_MODEL, 2 * X_WIDTH)),
        ],
        out_specs=[
            pl.BlockSpec((1, X_WIDTH, M), lambda b: (b, 0, 0)),
            pl.BlockSpec((1, M, X_WIDTH), lambda b: (b, 0, 0)),
        ],
        out_shape=[
            jax.ShapeDtypeStruct((B, X_WIDTH, M), BF16),
            jax.ShapeDtypeStruct((B, M, X_WIDTH), BF16),
        ],
        compiler_params=pltpu.CompilerParams(
            dimension_semantics=("arbitrary",), vmem_limit_bytes=VMEM_LIMIT),
        name="memkv",
    )(mem, ln_mem, w_mem_kv)


def _attn_kernel(qT_ref, k_ref, vT_ref, o_ref, acc_ref, m_ref, *, tk, n_chunks):
    acc_ref[...] = jnp.zeros_like(acc_ref)
    m_ref[...] = jnp.full_like(m_ref, -jnp.inf)

    def chunk(c, carry):
        start = pl.multiple_of(c * tk, tk)
        ks = k_ref[0, 0, pl.ds(start, tk), :]
        vs = vT_ref[0, 0, c]
        for hh in range(HEADS_PER_KV):
            s = jnp.dot(ks, qT_ref[0, hh], preferred_element_type=F32)
            m_old = m_ref[hh]
            m_new = jnp.maximum(m_old, jnp.max(s, axis=0, keepdims=True))
            p = jnp.exp(s - m_new).astype(BF16)
            alpha = jnp.exp(m_old - m_new)
            acc_ref[hh] = acc_ref[hh] * alpha + jnp.dot(vs, p, preferred_element_type=F32)
            m_ref[hh] = m_new
        return carry

    lax.fori_loop(0, n_chunks, chunk, 0)

    for hp in range(HEADS_PER_KV // 2):
        outs = []
        for hh in (2 * hp, 2 * hp + 1):
            a = acc_ref[hh]
            outs.append(a[:HEAD_DIM] / a[HEAD_DIM:HEAD_DIM + 1])
        o = jnp.concatenate(outs, axis=0).T
        o_ref[0, :, 128 * hp:128 * (hp + 1)] = o.astype(BF16)


def _attn_call(qT, kk, vT, tq):
    B, _, _, L = qT.shape
    n_chunks, tk = vT.shape[2], vT.shape[4]
    kern = functools.partial(_attn_kernel, tk=tk, n_chunks=n_chunks)
    return pl.pallas_call(
        kern,
        grid=(B, N_KV_HEADS, L // tq),
        in_specs=[
            pl.BlockSpec((1, HEADS_PER_KV, HEAD_DIM, tq), lambda b, g, i: (b, g, 0, i)),
            pl.BlockSpec((1, 1, L, HEAD_DIM), lambda b, g, i: (b, g, 0, 0)),
            pl.BlockSpec((1, 1, n_chunks, V_ROWS, tk), lambda b, g, i: (b, g, 0, 0, 0)),
        ],
        out_specs=pl.BlockSpec((1, tq, HEADS_PER_KV * HEAD_DIM), lambda b, g, i: (b, i, g)),
        out_shape=jax.ShapeDtypeStruct((B, L, ATTN_WIDTH), BF16),
        scratch_shapes=[
            pltpu.VMEM((HEADS_PER_KV, V_ROWS, tq), F32),
            pltpu.VMEM((HEADS_PER_KV, 1, tq), F32),
        ],
        compiler_params=pltpu.CompilerParams(
            dimension_semantics=("arbitrary", "arbitrary", "arbitrary"), vmem_limit_bytes=VMEM_LIMIT),
        name="attn",
    )(qT, kk, vT)


def _sigmoid(t):
    return 1.0 / (1.0 + jnp.exp(-t))


def _silu(t):
    return t * _sigmoid(t)


def _shift_rows(a, k):
    return pltpu.roll(a, k % a.shape[0], axis=0)


def _out_kernel(x_ref, zr_ref, prev_ref, next_ref, att_ref, mkT_ref, mv_ref, wpool_ref, pscale_ref,
                wbr_ref, bmerge_ref, wout_ref, lnpost_ref, y_ref, *, tm, seq_len):
    i = pl.program_id(1)
    nblk = pl.num_programs(1)

    prev_ok = (i > 0).astype(F32)
    next_ok = (i < nblk - 1).astype(F32)
    u_cur = zr_ref[0, :, R_POOL_IN:R_POOL_IN + POOL_WIDTH].astype(F32)
    u_ext = jnp.concatenate([prev_ref[0].astype(F32) * prev_ok, u_cur, next_ref[0].astype(F32) * next_ok],
                            axis=0)
    t = i * tm + lax.broadcasted_iota(jnp.int32, (tm, POOL_GROUP_DIM), 0)
    pool_gate = zr_ref[0, :, R_POOL_GATE:R_POOL_GATE + POOL_WIDTH].astype(F32)
    pooled_parts = []
    for g, w in enumerate(POOL_WINDOWS):
        a = u_ext[:, g * POOL_GROUP_DIM:(g + 1) * POOL_GROUP_DIM]
        ssum = a + _shift_rows(a, 1)
        reach = 1
        while 2 * reach < w:
            ssum = _shift_rows(ssum, reach) + _shift_rows(ssum, -reach)
            reach *= 2
        ssum = ssum[HALO:HALO + tm]
        lo = jnp.maximum(t - w // 2, 0)
        hi = jnp.minimum(t + (w - w // 2), seq_len)
        cnt = (hi - lo).astype(F32)
        mixed = (ssum / cnt - a[HALO:HALO + tm]).astype(BF16)
        pooled_parts.append(jnp.dot(mixed, wpool_ref[g], preferred_element_type=F32))
    pool_out = jnp.concatenate(pooled_parts, axis=1) * pscale_ref[...]
    br_pool = (pool_out * _silu(pool_gate)).astype(BF16)

    attn_gate = zr_ref[0, :, R_ATTN_GATE:R_ATTN_GATE + ATTN_WIDTH].astype(F32)
    br_attn = (att_ref[0].astype(F32) * _silu(attn_gate)).astype(BF16)

    x_scale = X_HEAD_DIM ** -0.5
    cross_parts = []
    for hh in range(N_X_HEADS):
        lo_c = hh * X_HEAD_DIM
        xq = zr_ref[0, :, R_XQ + lo_c:R_XQ + lo_c + X_HEAD_DIM]
        s = jnp.dot(xq, mkT_ref[0, lo_c:lo_c + X_HEAD_DIM, :], preferred_element_type=F32) * x_scale
        p = jnp.exp(s - jnp.max(s, axis=-1, keepdims=True))
        den = jnp.sum(p, axis=-1, keepdims=True)
        o = jnp.dot(p.astype(BF16), mv_ref[0, :, lo_c:lo_c + X_HEAD_DIM], preferred_element_type=F32)
        cross_parts.append(o / den)
    x_gate = zr_ref[0, :, R_X_GATE:R_X_GATE + X_WIDTH].astype(F32)
    br_cross = (jnp.concatenate(cross_parts, axis=1) * _silu(x_gate)).astype(BF16)

    merged = None
    for n, br in enumerate((br_pool, br_attn, br_cross)):
        proj = jnp.dot(br, wbr_ref[n], preferred_element_type=F32)
        logits = zr_ref[0, :, R_MERGE + n * D_MODEL:R_MERGE + (n + 1) * D_MODEL].astype(F32)
        gated = _sigmoid(logits + bmerge_ref[n:n + 1, :]) * proj
        merged = gated if merged is None else merged + gated

    o = jnp.dot(merged.astype(BF16), wout_ref[...], preferred_element_type=F32)
    ms = jnp.mean(o * o, axis=-1, keepdims=True)
    y_ref[0] = x_ref[0] + o * lax.rsqrt(ms + EPS) * lnpost_ref[...]


def _out_call(x, zr, att, mkT, mv, w_pool, pool_scale, w_branch, b_merge, w_out, ln_post, tm):
    B, L, _ = x.shape
    M = mv.shape[1]
    hb = tm // HALO
    last_halo = L // HALO - 1
    kern = functools.partial(_out_kernel, tm=tm, seq_len=L)
    return pl.pallas_call(
        kern,
        grid=(B, L // tm),
        in_specs=[
            pl.BlockSpec((1, tm, D_MODEL), lambda b, i: (b, i, 0)),
            pl.BlockSpec((1, tm, REST_WIDTH), lambda b, i: (b, i, 0)),
            pl.BlockSpec((1, HALO, POOL_WIDTH), lambda b, i: (b, jnp.maximum(i * hb - 1, 0), 0)),
            pl.BlockSpec((1, HALO, POOL_WIDTH), lambda b, i: (b, jnp.minimum((i + 1) * hb, last_halo), 0)),
            pl.BlockSpec((1, tm, ATTN_WIDTH), lambda b, i: (b, i, 0)),
            pl.BlockSpec((1, X_WIDTH, M), lambda b, i: (b, 0, 0)),
            pl.BlockSpec((1, M, X_WIDTH), lambda b, i: (b, 0, 0)),
            _const_spec((POOL_GROUPS, POOL_GROUP_DIM, POOL_GROUP_DIM)),
            _const_spec((1, POOL_WIDTH)),
            _const_spec((N_BRANCH, BRANCH_WIDTH, D_MODEL)),
            _const_spec((N_BRANCH, D_MODEL)),
            _const_spec((D_MODEL, D_MODEL)),
            _const_spec((1, D_MODEL)),
        ],
        out_specs=pl.BlockSpec((1, tm, D_MODEL), lambda b, i: (b, i, 0)),
        out_shape=jax.ShapeDtypeStruct((B, L, D_MODEL), F32),
        compiler_params=pltpu.CompilerParams(
            dimension_semantics=("arbitrary", "arbitrary"), vmem_limit_bytes=VMEM_LIMIT),
        name="mixout",
    )(x, zr, zr, zr, att, mkT, mv, w_pool, pool_scale, w_branch, b_merge, w_out, ln_post)


def _rope_tables(L):
    pos = jnp.arange(L, dtype=jnp.int32)
    row = (pos // GRID_W).astype(F32)
    col = (pos % GRID_W).astype(F32)
    axis_dim = HEAD_DIM // 2
    inv = ROPE_THETA ** (-jnp.arange(0, axis_dim, 2, dtype=F32) / axis_dim)
    ang = jnp.concatenate([row[:, None] * inv, col[:, None] * inv], axis=-1)
    cos = jnp.repeat(jnp.cos(ang), 2, axis=-1)
    sin = jnp.repeat(jnp.sin(ang), 2, axis=-1) * jnp.tile(jnp.array([-1.0, 1.0], F32), HEAD_DIM // 2)
    return jnp.tile(cos, (1, 2)), jnp.tile(sin, (1, 2))


def _layer(x, mem, prm, tm, tq):
    B, L, _ = x.shape
    cos_t, sin_t = _rope_tables(L)
    qT, kk, vT, zr = _proj_call(x, prm["ln_pre"], prm["w_qkv"], prm["w_rest"], prm["bd"], prm["qn"], prm["kn"],
                                cos_t, sin_t, tm)
    mkT, mv = _mem_call(mem, prm["ln_mem"], prm["w_mem_kv"])
    att = _attn_call(qT, kk, vT, tq)
    return _out_call(x, zr, att, mkT, mv, prm["w_pool"], prm["pool_scale"], prm["w_branch"], prm["b_merge"],
                     prm["w_out"], prm["ln_post"], tm)


def _prep_params(ln_pre, ln_post, ln_mem, w_in, b_merge, q_norm, k_norm, w_pool, pool_scale, w_mem_kv,
                 w_branch, w_out):
    w_in_b = w_in.astype(BF16)
    head_id = np.arange(ATTN_WIDTH) // HEAD_DIM
    bd = (head_id[:, None] == head_id[None, :]).astype(np.float32) / HEAD_DIM
    return {
        "ln_pre": ln_pre.reshape(1, D_MODEL),
        "ln_post": ln_post.reshape(1, D_MODEL),
        "ln_mem": ln_mem.reshape(1, D_MODEL),
        "w_qkv": w_in_b[:, _Q0:_AG0],
        "w_rest": jnp.concatenate([w_in_b[:, :_Q0], w_in_b[:, _AG0:]], axis=1),
        "bd": jnp.asarray(bd, BF16),
        "qn": jnp.tile(q_norm, N_HEADS).reshape(1, ATTN_WIDTH),
        "kn": jnp.tile(k_norm, N_KV_HEADS).reshape(1, KV_WIDTH),
        "w_pool": w_pool.astype(BF16),
        "pool_scale": pool_scale.reshape(1, POOL_WIDTH),
        "w_mem_kv": w_mem_kv.astype(BF16),
        "w_branch": w_branch.astype(BF16),
        "b_merge": b_merge,
        "w_out": w_out.astype(BF16),
    }


def kernel(x_prompt, x_sample, mem_prompt, mem_sample, ln_pre, ln_post, ln_mem, w_in, b_merge, q_norm, k_norm,
           w_pool, pool_scale, w_mem_kv, w_branch, w_out):
    prm = _prep_params(ln_pre, ln_post, ln_mem, w_in, b_merge, q_norm, k_norm, w_pool, pool_scale, w_mem_kv,
                       w_branch, w_out)
    outs = []
    for x, mem in ((x_prompt, mem_prompt), (x_sample, mem_sample)):
        L = x.shape[1]
        outs.append(_layer(x, mem, prm, tm=min(512, L), tq=min(256, L)))
    return tuple(outs)
```

```python
import functools

import numpy as np
import jax
import jax.numpy as jnp
from jax import lax
from jax.experimental import pallas as pl
from jax.experimental.pallas import tpu as pltpu

F32 = jnp.float32
BF16 = jnp.bfloat16

D_MODEL = 1024
GRID_W = 64
EPS = 1e-6
POOL_GROUPS = 4
POOL_GROUP_DIM = 128
POOL_WIDTH = 512
POOL_WINDOWS = (2, 4, 8, 16)
N_HEADS = 8
N_KV_HEADS = 2
HEAD_DIM = 64
HEADS_PER_KV = N_HEADS // N_KV_HEADS
ATTN_WIDTH = 512
KV_WIDTH = 128
ROPE_THETA = 10000.0
LOG2_E = 1.4426950408889634
N_X_HEADS = 4
X_HEAD_DIM = 128
X_WIDTH = 512
N_BRANCH = 3
BRANCH_WIDTH = 512

_Q0 = 2 * POOL_WIDTH
_K0 = _Q0 + ATTN_WIDTH
_V0 = _K0 + KV_WIDTH
_AG0 = _V0 + KV_WIDTH
QKV_WIDTH = ATTN_WIDTH + 2 * KV_WIDTH
R_POOL_IN = 0
R_POOL_GATE = 512
R_ATTN_GATE = 1024
R_XQ = 1536
R_X_GATE = 2048
R_MERGE = 2560
REST_WIDTH = R_MERGE + N_BRANCH * D_MODEL

V_ROWS = HEAD_DIM + 16
HALO = 16
VMEM_LIMIT = 56 * 1024 * 1024


def _const_spec(shape):
    n = len(shape)
    return pl.BlockSpec(shape, lambda *_: (0,) * n, pipeline_mode=pl.Buffered(1))


def _rope(t, c, s):
    lane = lax.broadcasted_iota(jnp.int32, t.shape, 1)
    nxt = pltpu.roll(t, t.shape[1] - 1, axis=1)
    prv = pltpu.roll(t, 1, axis=1)
    return t * c + jnp.where(lane % 2 == 0, nxt, prv) * s


def _proj_kernel(x_ref, lnpre_ref, wqkv_ref, wrest_ref, bd_ref, qn_ref, kn_ref, cos_ref, sin_ref,
                 qT_ref, k_ref, vT_ref, zr_ref):
    x = x_ref[0]
    ms = jnp.mean(x * x, axis=-1, keepdims=True)
    h = (x * lax.rsqrt(ms + EPS) * lnpre_ref[...]).astype(BF16)

    for c0 in range(0, REST_WIDTH, 512):
        zr_ref[0, :, c0:c0 + 512] = jnp.dot(
            h, wrest_ref[:, c0:c0 + 512], preferred_element_type=F32).astype(BF16)

    qkv = jnp.dot(h, wqkv_ref[...], preferred_element_type=F32)
    q = qkv[:, :ATTN_WIDTH]
    k = qkv[:, ATTN_WIDTH:ATTN_WIDTH + KV_WIDTH]
    v = qkv[:, ATTN_WIDTH + KV_WIDTH:]

    bd = bd_ref[...]
    q_ms = jnp.dot((q * q).astype(BF16), bd, preferred_element_type=F32)
    k_ms = jnp.dot((k * k).astype(BF16), bd[:KV_WIDTH, :KV_WIDTH], preferred_element_type=F32)
    qn = q * lax.rsqrt(q_ms + EPS) * qn_ref[...]
    kn = k * lax.rsqrt(k_ms + EPS) * kn_ref[...]

    c = cos_ref[...]
    s = sin_ref[...]
    scale = LOG2_E * HEAD_DIM ** -0.5
    for j in range(ATTN_WIDTH // 128):
        qt = (_rope(qn[:, 128 * j:128 * (j + 1)], c, s) * scale).T
        qT_ref[0, 2 * j] = qt[:HEAD_DIM].astype(BF16)
        qT_ref[0, 2 * j + 1] = qt[HEAD_DIM:].astype(BF16)
    kr = _rope(kn, c, s).astype(BF16)
    k_ref[0, 0] = kr[:, :HEAD_DIM]
    k_ref[0, 1] = kr[:, HEAD_DIM:]
    vt = v.T.astype(BF16)
    tk = vT_ref.shape[4]
    ones = jnp.ones((V_ROWS - HEAD_DIM, tk), BF16)
    for g in range(N_KV_HEADS):
        for cc in range(vT_ref.shape[2]):
            vT_ref[0, g, cc, :HEAD_DIM] = vt[g * HEAD_DIM:(g + 1) * HEAD_DIM, cc * tk:(cc + 1) * tk]
            vT_ref[0, g, cc, HEAD_DIM:] = ones


def _proj_call(x, ln_pre, w_qkv, w_rest, bd, qn, kn, cos_t, sin_t, tm, tk):
    B, L, _ = x.shape
    nblk = L // tm
    cpb = tm // tk
    return pl.pallas_call(
        _proj_kernel,
        grid=(B, nblk),
        in_specs=[
            pl.BlockSpec((1, tm, D_MODEL), lambda b, i: (b, i, 0)),
            _const_spec((1, D_MODEL)),
            _const_spec((D_MODEL, QKV_WIDTH)),
            _const_spec((D_MODEL, REST_WIDTH)),
            _const_spec((ATTN_WIDTH, ATTN_WIDTH)),
            _const_spec((1, ATTN_WIDTH)),
            _const_spec((1, KV_WIDTH)),
            pl.BlockSpec((tm, 128), lambda b, i: (i, 0)),
            pl.BlockSpec((tm, 128), lambda b, i: (i, 0)),
        ],
        out_specs=[
            pl.BlockSpec((1, N_HEADS, HEAD_DIM, tm), lambda b, i: (b, 0, 0, i)),
            pl.BlockSpec((1, N_KV_HEADS, tm, HEAD_DIM), lambda b, i: (b, 0, i, 0)),
            pl.BlockSpec((1, N_KV_HEADS, cpb, V_ROWS, tk), lambda b, i: (b, 0, i, 0, 0)),
            pl.BlockSpec((1, tm, REST_WIDTH), lambda b, i: (b, i, 0)),
        ],
        out_shape=[
            jax.ShapeDtypeStruct((B, N_HEADS, HEAD_DIM, L), BF16),
            jax.ShapeDtypeStruct((B, N_KV_HEADS, L, HEAD_DIM), BF16),
            jax.ShapeDtypeStruct((B, N_KV_HEADS, L // tk, V_ROWS, tk), BF16),
            jax.ShapeDtypeStruct((B, L, REST_WIDTH), BF16),
        ],
        compiler_params=pltpu.CompilerParams(
            dimension_semantics=("arbitrary", "arbitrary"), vmem_limit_bytes=VMEM_LIMIT),
        name="proj",
    )(x, ln_pre, w_qkv, w_rest, bd, qn, kn, cos_t, sin_t)


def _mem_kernel(mem_ref, lnmem_ref, wkv_ref, mkT_ref, mv_ref):
    m = mem_ref[0]
    ms = jnp.mean(m * m, axis=-1, keepdims=True)
    mn = (m * lax.rsqrt(ms + EPS) * lnmem_ref[...]).astype(BF16)
    kv = jnp.dot(mn, wkv_ref[...], preferred_element_type=F32)
    mkT_ref[0] = kv[:, :X_WIDTH].T.astype(BF16)
    mv_ref[0] = kv[:, X_WIDTH:].astype(BF16)


def _mem_call(mem, ln_mem, w_mem_kv):
    B, M, _ = mem.shape
    return pl.pallas_call(
        _mem_kernel,
        grid=(B,),
        in_specs=[
            pl.BlockSpec((1, M, D_MODEL), lambda b: (b, 0, 0)),
            _const_spec((1, D_MODEL)),
            _const_spec((D_MODEL, 2 * X_WIDTH)),
        ],
        out_specs=[
            pl.BlockSpec((1, X_WIDTH, M), lambda b: (b, 0, 0)),
            pl.BlockSpec((1, M, X_WIDTH), lambda b: (b, 0, 0)),
        ],
        out_shape=[
            jax.ShapeDtypeStruct((B, X_WIDTH, M), BF16),
            jax.ShapeDtypeStruct((B, M, X_WIDTH), BF16),
        ],
        compiler_params=pltpu.CompilerParams(
            dimension_semantics=("arbitrary",), vmem_limit_bytes=VMEM_LIMIT),
        name="memkv",
    )(mem, ln_mem, w_mem_kv)


def _attn_kernel(qT_ref, k_ref, vT_ref, o_ref, acc_ref, m_ref, s_ref, p_ref, alpha_ref, cmax_ref, *, tk,
                 n_chunks):
    acc_ref[...] = jnp.zeros_like(acc_ref)
    m_ref[...] = jnp.full_like(m_ref, -jnp.inf)

    def scores(c, hh):
        start = pl.multiple_of(c * tk, tk)
        ks = k_ref[0, 0, pl.ds(start, tk), :]
        s = jnp.dot(ks, qT_ref[0, hh], preferred_element_type=F32)
        s_ref[hh % 2] = s
        cmax_ref[hh % 2] = jnp.max(s, axis=0, keepdims=True)

    def softmax(hh):
        m_old = m_ref[hh]
        m_new = jnp.maximum(m_old, cmax_ref[hh % 2])
        p_ref[hh % 2] = jnp.exp2(s_ref[hh % 2] - m_new).astype(BF16)
        alpha_ref[hh % 2] = jnp.exp2(m_old - m_new)
        m_ref[hh] = m_new

    def accumulate(c, hh):
        pv = jnp.dot(vT_ref[0, 0, c], p_ref[hh % 2], preferred_element_type=F32)
        acc_ref[hh] = acc_ref[hh] * alpha_ref[hh % 2] + pv

    def chunk_body(c, last):
        for hh in range(HEADS_PER_KV):
            if hh + 2 < HEADS_PER_KV:
                scores(c, hh + 2)
            elif not last:
                scores(c + 1, hh + 2 - HEADS_PER_KV)
            if hh + 1 < HEADS_PER_KV:
                softmax(hh + 1)
            elif not last:
                softmax(0)
            accumulate(c, hh)

    scores(0, 0)
    scores(0, 1)
    softmax(0)

    def chunk(c, carry):
        chunk_body(c, last=False)
        return carry

    lax.fori_loop(0, n_chunks - 1, chunk, 0)
    chunk_body(n_chunks - 1, last=True)

    for hp in range(HEADS_PER_KV // 2):
        outs = []
        for hh in (2 * hp, 2 * hp + 1):
            a = acc_ref[hh]
            outs.append(a[:HEAD_DIM] / a[HEAD_DIM:HEAD_DIM + 1])
        o = jnp.concatenate(outs, axis=0).T
        o_ref[0, :, 128 * hp:128 * (hp + 1)] = o.astype(BF16)


def _attn_call(qT, kk, vT, tq):
    B, _, _, L = qT.shape
    n_chunks, tk = vT.shape[2], vT.shape[4]
    kern = functools.partial(_attn_kernel, tk=tk, n_chunks=n_chunks)
    return pl.pallas_call(
        kern,
        grid=(B, N_KV_HEADS, L // tq),
        in_specs=[
            pl.BlockSpec((1, HEADS_PER_KV, HEAD_DIM, tq), lambda b, g, i: (b, g, 0, i)),
            pl.BlockSpec((1, 1, L, HEAD_DIM), lambda b, g, i: (b, g, 0, 0)),
            pl.BlockSpec((1, 1, n_chunks, V_ROWS, tk), lambda b, g, i: (b, g, 0, 0, 0)),
        ],
        out_specs=pl.BlockSpec((1, tq, HEADS_PER_KV * HEAD_DIM), lambda b, g, i: (b, i, g)),
        out_shape=jax.ShapeDtypeStruct((B, L, ATTN_WIDTH), BF16),
        scratch_shapes=[
            pltpu.VMEM((HEADS_PER_KV, V_ROWS, tq), F32),
            pltpu.VMEM((HEADS_PER_KV, 1, tq), F32),
            pltpu.VMEM((2, tk, tq), F32),
            pltpu.VMEM((2, tk, tq), BF16),
            pltpu.VMEM((2, 1, tq), F32),
            pltpu.VMEM((2, 1, tq), F32),
        ],
        compiler_params=pltpu.CompilerParams(
            dimension_semantics=("arbitrary", "arbitrary", "arbitrary"), vmem_limit_bytes=VMEM_LIMIT),
        name="attn",
    )(qT, kk, vT)


def _sigmoid(t):
    return 1.0 / (1.0 + jnp.exp(-t))


def _silu(t):
    return t * _sigmoid(t)


def _shift_rows(a, k):
    return pltpu.roll(a, k % a.shape[0], axis=0)


def _out_kernel(x_ref, zr_ref, prev_ref, next_ref, att_ref, mkT_ref, mv_ref, wpool_ref, pscale_ref,
                wbr_ref, bmerge_ref, wout_ref, lnpost_ref, y_ref, *, tm, seq_len):
    i = pl.program_id(1)
    nblk = pl.num_programs(1)

    prev_ok = (i > 0).astype(F32)
    next_ok = (i < nblk - 1).astype(F32)
    u_cur = zr_ref[0, :, R_POOL_IN:R_POOL_IN + POOL_WIDTH].astype(F32)
    u_ext = jnp.concatenate([prev_ref[0].astype(F32) * prev_ok, u_cur, next_ref[0].astype(F32) * next_ok],
                            axis=0)
    t = i * tm + lax.broadcasted_iota(jnp.int32, (tm, POOL_GROUP_DIM), 0)
    pool_gate = zr_ref[0, :, R_POOL_GATE:R_POOL_GATE + POOL_WIDTH].astype(F32)
    pooled_parts = []
    for g, w in enumerate(POOL_WINDOWS):
        a = u_ext[:, g * POOL_GROUP_DIM:(g + 1) * POOL_GROUP_DIM]
        ssum = a + _shift_rows(a, 1)
        reach = 1
        while 2 * reach < w:
            ssum = _shift_rows(ssum, reach) + _shift_rows(ssum, -reach)
            reach *= 2
        ssum = ssum[HALO:HALO + tm]
        lo = jnp.maximum(t - w // 2, 0)
        hi = jnp.minimum(t + (w - w // 2), seq_len)
        cnt = (hi - lo).astype(F32)
        mixed = (ssum / cnt - a[HALO:HALO + tm]).astype(BF16)
        pooled_parts.append(jnp.dot(mixed, wpool_ref[g], preferred_element_type=F32))
    pool_out = jnp.concatenate(pooled_parts, axis=1) * pscale_ref[...]
    br_pool = (pool_out * _silu(pool_gate)).astype(BF16)

    attn_gate = zr_ref[0, :, R_ATTN_GATE:R_ATTN_GATE + ATTN_WIDTH].astype(F32)
    br_attn = (att_ref[0].astype(F32) * _silu(attn_gate)).astype(BF16)

    x_scale = X_HEAD_DIM ** -0.5
    cross_parts = []
    for hh in range(N_X_HEADS):
        lo_c = hh * X_HEAD_DIM
        xq = zr_ref[0, :, R_XQ + lo_c:R_XQ + lo_c + X_HEAD_DIM]
        s = jnp.dot(xq, mkT_ref[0, lo_c:lo_c + X_HEAD_DIM, :], preferred_element_type=F32) * x_scale
        p = jnp.exp(s - jnp.max(s, axis=-1, keepdims=True))
        den = jnp.sum(p, axis=-1, keepdims=True)
        o = jnp.dot(p.astype(BF16), mv_ref[0, :, lo_c:lo_c + X_HEAD_DIM], preferred_element_type=F32)
        cross_parts.append(o / den)
    x_gate = zr_ref[0, :, R_X_GATE:R_X_GATE + X_WIDTH].astype(F32)
    br_cross = (jnp.concatenate(cross_parts, axis=1) * _silu(x_gate)).astype(BF16)

    merged = None
    for n, br in enumerate((br_pool, br_attn, br_cross)):
        proj = jnp.dot(br, wbr_ref[n], preferred_element_type=F32)
        logits = zr_ref[0, :, R_MERGE + n * D_MODEL:R_MERGE + (n + 1) * D_MODEL].astype(F32)
        gated = _sigmoid(logits + bmerge_ref[n:n + 1, :]) * proj
        merged = gated if merged is None else merged + gated

    o = jnp.dot(merged.astype(BF16), wout_ref[...], preferred_element_type=F32)
    ms = jnp.mean(o * o, axis=-1, keepdims=True)
    y_ref[0] = x_ref[0] + o * lax.rsqrt(ms + EPS) * lnpost_ref[...]


def _out_call(x, zr, att, mkT, mv, w_pool, pool_scale, w_branch, b_merge, w_out, ln_post, tm):
    B, L, _ = x.shape
    M = mv.shape[1]
    hb = tm // HALO
    last_halo = L // HALO - 1
    kern = functools.partial(_out_kernel, tm=tm, seq_len=L)
    return pl.pallas_call(
        kern,
        grid=(B, L // tm),
        in_specs=[
            pl.BlockSpec((1, tm, D_MODEL), lambda b, i: (b, i, 0)),
            pl.BlockSpec((1, tm, REST_WIDTH), lambda b, i: (b, i, 0)),
            pl.BlockSpec((1, HALO, POOL_WIDTH), lambda b, i: (b, jnp.maximum(i * hb - 1, 0), 0)),
            pl.BlockSpec((1, HALO, POOL_WIDTH), lambda b, i: (b, jnp.minimum((i + 1) * hb, last_halo), 0)),
            pl.BlockSpec((1, tm, ATTN_WIDTH), lambda b, i: (b, i, 0)),
            pl.BlockSpec((1, X_WIDTH, M), lambda b, i: (b, 0, 0)),
            pl.BlockSpec((1, M, X_WIDTH), lambda b, i: (b, 0, 0)),
            _const_spec((POOL_GROUPS, POOL_GROUP_DIM, POOL_GROUP_DIM)),
            _const_spec((1, POOL_WIDTH)),
            _const_spec((N_BRANCH, BRANCH_WIDTH, D_MODEL)),
            _const_spec((N_BRANCH, D_MODEL)),
            _const_spec((D_MODEL, D_MODEL)),
            _const_spec((1, D_MODEL)),
        ],
        out_specs=pl.BlockSpec((1, tm, D_MODEL), lambda b, i: (b, i, 0)),
        out_shape=jax.ShapeDtypeStruct((B, L, D_MODEL), F32),
        compiler_params=pltpu.CompilerParams(
            dimension_semantics=("arbitrary", "arbitrary"), vmem_limit_bytes=VMEM_LIMIT),
        name="mixout",
    )(x, zr, zr, zr, att, mkT, mv, w_pool, pool_scale, w_branch, b_merge, w_out, ln_post)


def _rope_tables(L):
    pos = jnp.arange(L, dtype=jnp.int32)
    row = (pos // GRID_W).astype(F32)
    col = (pos % GRID_W).astype(F32)
    axis_dim = HEAD_DIM // 2
    inv = ROPE_THETA ** (-jnp.arange(0, axis_dim, 2, dtype=F32) / axis_dim)
    ang = jnp.concatenate([row[:, None] * inv, col[:, None] * inv], axis=-1)
    cos = jnp.repeat(jnp.cos(ang), 2, axis=-1)
    sin = jnp.repeat(jnp.sin(ang), 2, axis=-1) * jnp.tile(jnp.array([-1.0, 1.0], F32), HEAD_DIM // 2)
    return jnp.tile(cos, (1, 2)), jnp.tile(sin, (1, 2))


def _layer(x, mem, prm, tm, tq, tk):
    B, L, _ = x.shape
    cos_t, sin_t = _rope_tables(L)
    qT, kk, vT, zr = _proj_call(x, prm["ln_pre"], prm["w_qkv"], prm["w_rest"], prm["bd"], prm["qn"], prm["kn"],
                                cos_t, sin_t, tm, tk)
    mkT, mv = _mem_call(mem, prm["ln_mem"], prm["w_mem_kv"])
    att = _attn_call(qT, kk, vT, tq)
    return _out_call(x, zr, att, mkT, mv, prm["w_pool"], prm["pool_scale"], prm["w_branch"], prm["b_merge"],
                     prm["w_out"], prm["ln_post"], tm)


def _prep_params(ln_pre, ln_post, ln_mem, w_in, b_merge, q_norm, k_norm, w_pool, pool_scale, w_mem_kv,
                 w_branch, w_out):
    w_in_b = w_in.astype(BF16)
    head_id = np.arange(ATTN_WIDTH) // HEAD_DIM
    bd = (head_id[:, None] == head_id[None, :]).astype(np.float32) / HEAD_DIM
    return {
        "ln_pre": ln_pre.reshape(1, D_MODEL),
        "ln_post": ln_post.reshape(1, D_MODEL),
        "ln_mem": ln_mem.reshape(1, D_MODEL),
        "w_qkv": w_in_b[:, _Q0:_AG0],
        "w_rest": jnp.concatenate([w_in_b[:, :_Q0], w_in_b[:, _AG0:]], axis=1),
        "bd": jnp.asarray(bd, BF16),
        "qn": jnp.tile(q_norm, N_HEADS).reshape(1, ATTN_WIDTH),
        "kn": jnp.tile(k_norm, N_KV_HEADS).reshape(1, KV_WIDTH),
        "w_pool": w_pool.astype(BF16),
        "pool_scale": pool_scale.reshape(1, POOL_WIDTH),
        "w_mem_kv": w_mem_kv.astype(BF16),
        "w_branch": w_branch.astype(BF16),
        "b_merge": b_merge,
        "w_out": w_out.astype(BF16),
    }


def kernel(x_prompt, x_sample, mem_prompt, mem_sample, ln_pre, ln_post, ln_mem, w_in, b_merge, q_norm, k_norm,
           w_pool, pool_scale, w_mem_kv, w_branch, w_out):
    prm = _prep_params(ln_pre, ln_post, ln_mem, w_in, b_merge, q_norm, k_norm, w_pool, pool_scale, w_mem_kv,
                       w_branch, w_out)
    outs = []
    for x, mem in ((x_prompt, mem_prompt), (x_sample, mem_sample)):
        L = x.shape[1]
        outs.append(_layer(x, mem, prm, tm=min(512, L), tq=min(512, L), tk=min(256, L)))
    return tuple(outs)
```

```python
import functools

import numpy as np
import jax
import jax.numpy as jnp
from jax import lax
from jax.experimental import pallas as pl
from jax.experimental.pallas import tpu as pltpu

F32 = jnp.float32
BF16 = jnp.bfloat16

D_MODEL = 1024
GRID_W = 64
EPS = 1e-6
POOL_GROUPS = 4
POOL_GROUP_DIM = 128
POOL_WIDTH = 512
POOL_WINDOWS = (2, 4, 8, 16)
N_HEADS = 8
N_KV_HEADS = 2
HEAD_DIM = 64
HEADS_PER_KV = N_HEADS // N_KV_HEADS
ATTN_WIDTH = 512
KV_WIDTH = 128
ROPE_THETA = 10000.0
LOG2_E = 1.4426950408889634
N_X_HEADS = 4
X_HEAD_DIM = 128
X_WIDTH = 512
N_BRANCH = 3
BRANCH_WIDTH = 512

_Q0 = 2 * POOL_WIDTH
_K0 = _Q0 + ATTN_WIDTH
_V0 = _K0 + KV_WIDTH
_AG0 = _V0 + KV_WIDTH
QKV_WIDTH = ATTN_WIDTH + 2 * KV_WIDTH
R_POOL_IN = 0
R_POOL_GATE = 512
R_ATTN_GATE = 1024
R_XQ = 1536
R_X_GATE = 2048
R_MERGE = 2560
REST_WIDTH = R_MERGE + N_BRANCH * D_MODEL

V_ROWS = HEAD_DIM + 16
HALO = 16
VMEM_LIMIT = 56 * 1024 * 1024
CHUNK_UNROLL = 3


def _const_spec(shape):
    n = len(shape)
    return pl.BlockSpec(shape, lambda *_: (0,) * n, pipeline_mode=pl.Buffered(1))


def _rope(t, c, s):
    lane = lax.broadcasted_iota(jnp.int32, t.shape, 1)
    nxt = pltpu.roll(t, t.shape[1] - 1, axis=1)
    prv = pltpu.roll(t, 1, axis=1)
    return t * c + jnp.where(lane % 2 == 0, nxt, prv) * s


def _proj_kernel(x_ref, lnpre_ref, wqkv_ref, wrest_ref, bd_ref, qn_ref, kn_ref, cos_ref, sin_ref,
                 qT_ref, k_ref, vT_ref, zr_ref):
    x = x_ref[0]
    ms = jnp.mean(x * x, axis=-1, keepdims=True)
    h = (x * lax.rsqrt(ms + EPS) * lnpre_ref[...]).astype(BF16)

    qkv = jnp.dot(h, wqkv_ref[...], preferred_element_type=F32)
    q = qkv[:, :ATTN_WIDTH]
    k = qkv[:, ATTN_WIDTH:ATTN_WIDTH + KV_WIDTH]
    v = qkv[:, ATTN_WIDTH + KV_WIDTH:]

    bd = bd_ref[...]
    q_ms = jnp.dot((q * q).astype(BF16), bd, preferred_element_type=F32)
    k_ms = jnp.dot((k * k).astype(BF16), bd[:KV_WIDTH, :KV_WIDTH], preferred_element_type=F32)
    qn = q * lax.rsqrt(q_ms + EPS) * qn_ref[...]
    kn = k * lax.rsqrt(k_ms + EPS) * kn_ref[...]

    c = cos_ref[...]
    s = sin_ref[...]
    scale = LOG2_E * HEAD_DIM ** -0.5
    for j in range(ATTN_WIDTH // 128):
        qt = (_rope(qn[:, 128 * j:128 * (j + 1)], c, s) * scale).T
        qT_ref[0, 2 * j] = qt[:HEAD_DIM].astype(BF16)
        qT_ref[0, 2 * j + 1] = qt[HEAD_DIM:].astype(BF16)
    kr = _rope(kn, c, s).astype(BF16)
    k_ref[0, 0] = kr[:, :HEAD_DIM]
    k_ref[0, 1] = kr[:, HEAD_DIM:]
    vt = v.T.astype(BF16)
    tk = vT_ref.shape[4]
    ones = jnp.ones((V_ROWS - HEAD_DIM, tk), BF16)
    for g in range(N_KV_HEADS):
        for cc in range(vT_ref.shape[2]):
            vT_ref[0, g, cc, :HEAD_DIM] = vt[g * HEAD_DIM:(g + 1) * HEAD_DIM, cc * tk:(cc + 1) * tk]
            vT_ref[0, g, cc, HEAD_DIM:] = ones

    for c0 in range(0, REST_WIDTH, 512):
        zr_ref[0, :, c0:c0 + 512] = jnp.dot(
            h, wrest_ref[:, c0:c0 + 512], preferred_element_type=F32).astype(BF16)


def _proj_call(x, ln_pre, w_qkv, w_rest, bd, qn, kn, cos_t, sin_t, tm, tk):
    B, L, _ = x.shape
    nblk = L // tm
    cpb = tm // tk
    return pl.pallas_call(
        _proj_kernel,
        grid=(B, nblk),
        in_specs=[
            pl.BlockSpec((1, tm, D_MODEL), lambda b, i: (b, i, 0)),
            _const_spec((1, D_MODEL)),
            _const_spec((D_MODEL, QKV_WIDTH)),
            _const_spec((D_MODEL, REST_WIDTH)),
            _const_spec((ATTN_WIDTH, ATTN_WIDTH)),
            _const_spec((1, ATTN_WIDTH)),
            _const_spec((1, KV_WIDTH)),
            pl.BlockSpec((tm, 128), lambda b, i: (i, 0)),
            pl.BlockSpec((tm, 128), lambda b, i: (i, 0)),
        ],
        out_specs=[
            pl.BlockSpec((1, N_HEADS, HEAD_DIM, tm), lambda b, i: (b, 0, 0, i)),
            pl.BlockSpec((1, N_KV_HEADS, tm, HEAD_DIM), lambda b, i: (b, 0, i, 0)),
            pl.BlockSpec((1, N_KV_HEADS, cpb, V_ROWS, tk), lambda b, i: (b, 0, i, 0, 0)),
            pl.BlockSpec((1, tm, REST_WIDTH), lambda b, i: (b, i, 0)),
        ],
        out_shape=[
            jax.ShapeDtypeStruct((B, N_HEADS, HEAD_DIM, L), BF16),
            jax.ShapeDtypeStruct((B, N_KV_HEADS, L, HEAD_DIM), BF16),
            jax.ShapeDtypeStruct((B, N_KV_HEADS, L // tk, V_ROWS, tk), BF16),
            jax.ShapeDtypeStruct((B, L, REST_WIDTH), BF16),
        ],
        compiler_params=pltpu.CompilerParams(
            dimension_semantics=("arbitrary", "arbitrary"), vmem_limit_bytes=VMEM_LIMIT),
        name="proj",
    )(x, ln_pre, w_qkv, w_rest, bd, qn, kn, cos_t, sin_t)


def _mem_kernel(mem_ref, lnmem_ref, wkv_ref, mkT_ref, mv_ref):
    m = mem_ref[0]
    ms = jnp.mean(m * m, axis=-1, keepdims=True)
    mn = (m * lax.rsqrt(ms + EPS) * lnmem_ref[...]).astype(BF16)
    kv = jnp.dot(mn, wkv_ref[...], preferred_element_type=F32)
    mkT_ref[0] = kv[:, :X_WIDTH].T.astype(BF16)
    mv_ref[0] = kv[:, X_WIDTH:].astype(BF16)


def _mem_call(mem, ln_mem, w_mem_kv):
    B, M, _ = mem.shape
    return pl.pallas_call(
        _mem_kernel,
        grid=(B,),
        in_specs=[
            pl.BlockSpec((1, M, D_MODEL), lambda b: (b, 0, 0)),
            _const_spec((1, D_MODEL)),
            _const_spec((D_MODEL, 2 * X_WIDTH)),
        ],
        out_specs=[
            pl.BlockSpec((1, X_WIDTH, M), lambda b: (b, 0, 0)),
            pl.BlockSpec((1, M, X_WIDTH), lambda b: (b, 0, 0)),
        ],
        out_shape=[
            jax.ShapeDtypeStruct((B, X_WIDTH, M), BF16),
            jax.ShapeDtypeStruct((B, M, X_WIDTH), BF16),
        ],
        compiler_params=pltpu.CompilerParams(
            dimension_semantics=("arbitrary",), vmem_limit_bytes=VMEM_LIMIT),
        name="memkv",
    )(mem, ln_mem, w_mem_kv)


def _attn_kernel(qT_ref, k_ref, vT_ref, o_ref, acc_ref, m_ref, s_ref, p_ref, alpha_ref, cmax_ref, *, tk,
                 n_chunks):
    acc_ref[...] = jnp.zeros_like(acc_ref)
    m_ref[...] = jnp.full_like(m_ref, -jnp.inf)

    def scores(c, hh):
        start = pl.multiple_of(c * tk, tk)
        ks = k_ref[0, 0, pl.ds(start, tk), :]
        s = jnp.dot(ks, qT_ref[0, hh], preferred_element_type=F32)
        s_ref[hh % 2] = s
        cmax_ref[hh % 2] = jnp.max(s, axis=0, keepdims=True)

    def softmax(hh):
        m_old = m_ref[hh]
        m_new = jnp.maximum(m_old, cmax_ref[hh % 2])
        p_ref[hh % 2] = jnp.exp2(s_ref[hh % 2] - m_new).astype(BF16)
        alpha_ref[hh % 2] = jnp.exp2(m_old - m_new)
        m_ref[hh] = m_new

    def accumulate(c, hh):
        pv = jnp.dot(vT_ref[0, 0, c], p_ref[hh % 2], preferred_element_type=F32)
        acc_ref[hh] = acc_ref[hh] * alpha_ref[hh % 2] + pv

    def chunk_body(c, last):
        for hh in range(HEADS_PER_KV):
            if hh + 2 < HEADS_PER_KV:
                scores(c, hh + 2)
            elif not last:
                scores(c + 1, hh + 2 - HEADS_PER_KV)
            if hh + 1 < HEADS_PER_KV:
                softmax(hh + 1)
            elif not last:
                softmax(0)
            accumulate(c, hh)

    scores(0, 0)
    scores(0, 1)
    softmax(0)

    def chunk(c, carry):
        chunk_body(c, last=False)
        return carry

    lax.fori_loop(0, n_chunks - 1, chunk, 0, unroll=CHUNK_UNROLL)
    chunk_body(n_chunks - 1, last=True)

    for hp in range(HEADS_PER_KV // 2):
        outs = []
        for hh in (2 * hp, 2 * hp + 1):
            a = acc_ref[hh]
            outs.append(a[:HEAD_DIM] / a[HEAD_DIM:HEAD_DIM + 1])
        o = jnp.concatenate(outs, axis=0).T
        o_ref[0, :, 128 * hp:128 * (hp + 1)] = o.astype(BF16)


def _attn_call(qT, kk, vT, tq):
    B, _, _, L = qT.shape
    n_chunks, tk = vT.shape[2], vT.shape[4]
    kern = functools.partial(_attn_kernel, tk=tk, n_chunks=n_chunks)
    return pl.pallas_call(
        kern,
        grid=(B, N_KV_HEADS, L // tq),
        in_specs=[
            pl.BlockSpec((1, HEADS_PER_KV, HEAD_DIM, tq), lambda b, g, i: (b, g, 0, i)),
            pl.BlockSpec((1, 1, L, HEAD_DIM), lambda b, g, i: (b, g, 0, 0)),
            pl.BlockSpec((1, 1, n_chunks, V_ROWS, tk), lambda b, g, i: (b, g, 0, 0, 0)),
        ],
        out_specs=pl.BlockSpec((1, tq, HEADS_PER_KV * HEAD_DIM), lambda b, g, i: (b, i, g)),
        out_shape=jax.ShapeDtypeStruct((B, L, ATTN_WIDTH), BF16),
        scratch_shapes=[
            pltpu.VMEM((HEADS_PER_KV, V_ROWS, tq), F32),
            pltpu.VMEM((HEADS_PER_KV, 1, tq), F32),
            pltpu.VMEM((2, tk, tq), F32),
            pltpu.VMEM((2, tk, tq), BF16),
            pltpu.VMEM((2, 1, tq), F32),
            pltpu.VMEM((2, 1, tq), F32),
        ],
        compiler_params=pltpu.CompilerParams(
            dimension_semantics=("arbitrary", "arbitrary", "arbitrary"), vmem_limit_bytes=VMEM_LIMIT),
        name="attn",
    )(qT, kk, vT)


def _silu_of_half(hg):
    return hg + hg * jnp.tanh(hg)


def _shift_rows(a, k):
    return pltpu.roll(a, k % a.shape[0], axis=0)


def _out_kernel(x_ref, zr_ref, prev_ref, next_ref, att_ref, mkT_ref, mv_ref, wpool_ref, pscale_ref,
                wbr_ref, bmerge_ref, wout_ref, lnpost_ref, y_ref, *, tm, seq_len):
    i = pl.program_id(1)
    nblk = pl.num_programs(1)

    prev_ok = (i > 0).astype(F32)
    next_ok = (i < nblk - 1).astype(F32)
    u_cur = zr_ref[0, :, R_POOL_IN:R_POOL_IN + POOL_WIDTH].astype(F32)
    u_ext = jnp.concatenate([prev_ref[0].astype(F32) * prev_ok, u_cur, next_ref[0].astype(F32) * next_ok],
                            axis=0)
    t = i * tm + lax.broadcasted_iota(jnp.int32, (tm, POOL_GROUP_DIM), 0)
    pool_gate_h = zr_ref[0, :, R_POOL_GATE:R_POOL_GATE + POOL_WIDTH].astype(F32)
    pooled_parts = []
    for g, w in enumerate(POOL_WINDOWS):
        a = u_ext[:, g * POOL_GROUP_DIM:(g + 1) * POOL_GROUP_DIM]
        ssum = a + _shift_rows(a, 1)
        reach = 1
        while 2 * reach < w:
            ssum = _shift_rows(ssum, reach) + _shift_rows(ssum, -reach)
            reach *= 2
        ssum = ssum[HALO:HALO + tm]
        lo = jnp.maximum(t - w // 2, 0)
        hi = jnp.minimum(t + (w - w // 2), seq_len)
        cnt = (hi - lo).astype(F32)
        mixed = (ssum / cnt - a[HALO:HALO + tm]).astype(BF16)
        pooled_parts.append(jnp.dot(mixed, wpool_ref[g], preferred_element_type=F32))
    pool_out = jnp.concatenate(pooled_parts, axis=1) * pscale_ref[...]
    br_pool = (pool_out * _silu_of_half(pool_gate_h)).astype(BF16)

    attn_gate_h = zr_ref[0, :, R_ATTN_GATE:R_ATTN_GATE + ATTN_WIDTH].astype(F32)
    br_attn = (att_ref[0].astype(F32) * _silu_of_half(attn_gate_h)).astype(BF16)

    cross_parts = []
    for hh in range(N_X_HEADS):
        lo_c = hh * X_HEAD_DIM
        xq = zr_ref[0, :, R_XQ + lo_c:R_XQ + lo_c + X_HEAD_DIM]
        s = jnp.dot(xq, mkT_ref[0, lo_c:lo_c + X_HEAD_DIM, :], preferred_element_type=F32)
        p = jnp.exp2(s - jnp.max(s, axis=-1, keepdims=True))
        den = jnp.sum(p, axis=-1, keepdims=True)
        o = jnp.dot(p.astype(BF16), mv_ref[0, :, lo_c:lo_c + X_HEAD_DIM], preferred_element_type=F32)
        cross_parts.append(o / den)
    x_gate_h = zr_ref[0, :, R_X_GATE:R_X_GATE + X_WIDTH].astype(F32)
    br_cross = (jnp.concatenate(cross_parts, axis=1) * _silu_of_half(x_gate_h)).astype(BF16)

    merged2 = None
    for n, br in enumerate((br_pool, br_attn, br_cross)):
        proj = jnp.dot(br, wbr_ref[n], preferred_element_type=F32)
        logits_h = zr_ref[0, :, R_MERGE + n * D_MODEL:R_MERGE + (n + 1) * D_MODEL].astype(F32)
        gated2 = proj + jnp.tanh(logits_h + 0.5 * bmerge_ref[n:n + 1, :]) * proj
        merged2 = gated2 if merged2 is None else merged2 + gated2

    o = jnp.dot(merged2.astype(BF16), wout_ref[...], preferred_element_type=F32)
    ms = jnp.mean(o * o, axis=-1, keepdims=True)
    y_ref[0] = x_ref[0] + o * lax.rsqrt(ms + EPS) * lnpost_ref[...]


def _out_call(x, zr, att, mkT, mv, w_pool, pool_scale, w_branch, b_merge, w_out, ln_post, tm):
    B, L, _ = x.shape
    M = mv.shape[1]
    hb = tm // HALO
    last_halo = L // HALO - 1
    kern = functools.partial(_out_kernel, tm=tm, seq_len=L)
    return pl.pallas_call(
        kern,
        grid=(B, L // tm),
        in_specs=[
            pl.BlockSpec((1, tm, D_MODEL), lambda b, i: (b, i, 0)),
            pl.BlockSpec((1, tm, REST_WIDTH), lambda b, i: (b, i, 0)),
            pl.BlockSpec((1, HALO, POOL_WIDTH), lambda b, i: (b, jnp.maximum(i * hb - 1, 0), 0)),
            pl.BlockSpec((1, HALO, POOL_WIDTH), lambda b, i: (b, jnp.minimum((i + 1) * hb, last_halo), 0)),
            pl.BlockSpec((1, tm, ATTN_WIDTH), lambda b, i: (b, i, 0)),
            pl.BlockSpec((1, X_WIDTH, M), lambda b, i: (b, 0, 0)),
            pl.BlockSpec((1, M, X_WIDTH), lambda b, i: (b, 0, 0)),
            _const_spec((POOL_GROUPS, POOL_GROUP_DIM, POOL_GROUP_DIM)),
            _const_spec((1, POOL_WIDTH)),
            _const_spec((N_BRANCH, BRANCH_WIDTH, D_MODEL)),
            _const_spec((N_BRANCH, D_MODEL)),
            _const_spec((D_MODEL, D_MODEL)),
            _const_spec((1, D_MODEL)),
        ],
        out_specs=pl.BlockSpec((1, tm, D_MODEL), lambda b, i: (b, i, 0)),
        out_shape=jax.ShapeDtypeStruct((B, L, D_MODEL), F32),
        compiler_params=pltpu.CompilerParams(
            dimension_semantics=("arbitrary", "arbitrary"), vmem_limit_bytes=VMEM_LIMIT),
        name="mixout",
    )(x, zr, zr, zr, att, mkT, mv, w_pool, pool_scale, w_branch, b_merge, w_out, ln_post)


def _rope_tables(L):
    pos = jnp.arange(L, dtype=jnp.int32)
    row = (pos // GRID_W).astype(F32)
    col = (pos % GRID_W).astype(F32)
    axis_dim = HEAD_DIM // 2
    inv = ROPE_THETA ** (-jnp.arange(0, axis_dim, 2, dtype=F32) / axis_dim)
    ang = jnp.concatenate([row[:, None] * inv, col[:, None] * inv], axis=-1)
    cos = jnp.repeat(jnp.cos(ang), 2, axis=-1)
    sin = jnp.repeat(jnp.sin(ang), 2, axis=-1) * jnp.tile(jnp.array([-1.0, 1.0], F32), HEAD_DIM // 2)
    return jnp.tile(cos, (1, 2)), jnp.tile(sin, (1, 2))


def _layer(x, mem, prm, tm, tq, tk):
    B, L, _ = x.shape
    cos_t, sin_t = _rope_tables(L)
    qT, kk, vT, zr = _proj_call(x, prm["ln_pre"], prm["w_qkv"], prm["w_rest"], prm["bd"], prm["qn"], prm["kn"],
                                cos_t, sin_t, tm, tk)
    mkT, mv = _mem_call(mem, prm["ln_mem"], prm["w_mem_kv"])
    att = _attn_call(qT, kk, vT, tq)
    return _out_call(x, zr, att, mkT, mv, prm["w_pool"], prm["pool_scale"], prm["w_branch"], prm["b_merge"],
                     prm["w_out"], prm["ln_post"], tm)


def _prep_params(ln_pre, ln_post, ln_mem, w_in, b_merge, q_norm, k_norm, w_pool, pool_scale, w_mem_kv,
                 w_branch, w_out):
    head_id = np.arange(ATTN_WIDTH) // HEAD_DIM
    bd = (head_id[:, None] == head_id[None, :]).astype(np.float32) / HEAD_DIM
    col_scale = np.ones((REST_WIDTH,), np.float32)
    col_scale[R_POOL_GATE:R_POOL_GATE + POOL_WIDTH] = 0.5
    col_scale[R_ATTN_GATE:R_ATTN_GATE + ATTN_WIDTH] = 0.5
    col_scale[R_XQ:R_XQ + X_WIDTH] = LOG2_E * X_HEAD_DIM ** -0.5
    col_scale[R_X_GATE:R_X_GATE + X_WIDTH] = 0.5
    col_scale[R_MERGE:] = 0.5
    w_rest = jnp.concatenate([w_in[:, :_Q0], w_in[:, _AG0:]], axis=1) * col_scale
    return {
        "ln_pre": ln_pre.reshape(1, D_MODEL),
        "ln_post": ln_post.reshape(1, D_MODEL),
        "ln_mem": ln_mem.reshape(1, D_MODEL),
        "w_qkv": w_in[:, _Q0:_AG0].astype(BF16),
        "w_rest": w_rest.astype(BF16),
        "bd": jnp.asarray(bd, BF16),
        "qn": jnp.tile(q_norm, N_HEADS).reshape(1, ATTN_WIDTH),
        "kn": jnp.tile(k_norm, N_KV_HEADS).reshape(1, KV_WIDTH),
        "w_pool": w_pool.astype(BF16),
        "pool_scale": pool_scale.reshape(1, POOL_WIDTH),
        "w_mem_kv": w_mem_kv.astype(BF16),
        "w_branch": w_branch.astype(BF16),
        "b_merge": b_merge,
        "w_out": (0.5 * w_out).astype(BF16),
    }


def kernel(x_prompt, x_sample, mem_prompt, mem_sample, ln_pre, ln_post, ln_mem, w_in, b_merge, q_norm, k_norm,
           w_pool, pool_scale, w_mem_kv, w_branch, w_out):
    prm = _prep_params(ln_pre, ln_post, ln_mem, w_in, b_merge, q_norm, k_norm, w_pool, pool_scale, w_mem_kv,
                       w_branch, w_out)
    outs = []
    for x, mem in ((x_prompt, mem_prompt), (x_sample, mem_sample)):
        L = x.shape[1]
        outs.append(_layer(x, mem, prm, tm=min(512, L), tq=min(512, L), tk=min(256, L)))
    return tuple(outs)
```

```python
import functools

import numpy as np
import jax
import jax.numpy as jnp
from jax import lax
from jax.experimental import pallas as pl
from jax.experimental.pallas import tpu as pltpu

F32 = jnp.float32
BF16 = jnp.bfloat16

D_MODEL = 1024
GRID_W = 64
EPS = 1e-6
POOL_GROUPS = 4
POOL_GROUP_DIM = 128
POOL_WIDTH = 512
POOL_WINDOWS = (2, 4, 8, 16)
N_HEADS = 8
N_KV_HEADS = 2
HEAD_DIM = 64
HEADS_PER_KV = N_HEADS // N_KV_HEADS
ATTN_WIDTH = 512
KV_WIDTH = 128
ROPE_THETA = 10000.0
LOG2_E = 1.4426950408889634
N_X_HEADS = 4
X_HEAD_DIM = 128
X_WIDTH = 512
N_BRANCH = 3
BRANCH_WIDTH = 512

_Q0 = 2 * POOL_WIDTH
_K0 = _Q0 + ATTN_WIDTH
_V0 = _K0 + KV_WIDTH
_AG0 = _V0 + KV_WIDTH
QKV_WIDTH = ATTN_WIDTH + 2 * KV_WIDTH
R_POOL_IN = 0
R_POOL_GATE = 512
R_ATTN_GATE = 1024
R_XQ = 1536
R_X_GATE = 2048
R_MERGE = 2560
REST_WIDTH = R_MERGE + N_BRANCH * D_MODEL

V_ROWS = HEAD_DIM + 16
HALO = 16
VMEM_LIMIT = 56 * 1024 * 1024
CHUNK_UNROLL = 3
MAX_LEAD = 64.0


def _const_spec(shape):
    n = len(shape)
    return pl.BlockSpec(shape, lambda *_: (0,) * n, pipeline_mode=pl.Buffered(1))


def _rope(t, c, s):
    lane = lax.broadcasted_iota(jnp.int32, t.shape, 1)
    nxt = pltpu.roll(t, t.shape[1] - 1, axis=1)
    prv = pltpu.roll(t, 1, axis=1)
    return t * c + jnp.where(lane % 2 == 0, nxt, prv) * s


def _proj_kernel(x_ref, lnpre_ref, wqkv_ref, wrest_ref, bd_ref, qn_ref, kn_ref, cos_ref, sin_ref,
                 qT_ref, k_ref, vT_ref, zr_ref):
    x = x_ref[0]
    ms = jnp.mean(x * x, axis=-1, keepdims=True)
    h = (x * lax.rsqrt(ms + EPS) * lnpre_ref[...]).astype(BF16)

    qkv = jnp.dot(h, wqkv_ref[...], preferred_element_type=F32)
    q = qkv[:, :ATTN_WIDTH]
    k = qkv[:, ATTN_WIDTH:ATTN_WIDTH + KV_WIDTH]
    v = qkv[:, ATTN_WIDTH + KV_WIDTH:]

    bd = bd_ref[...]
    q_ms = jnp.dot((q * q).astype(BF16), bd, preferred_element_type=F32)
    k_ms = jnp.dot((k * k).astype(BF16), bd[:KV_WIDTH, :KV_WIDTH], preferred_element_type=F32)
    qn = q * lax.rsqrt(q_ms + EPS) * qn_ref[...]
    kn = k * lax.rsqrt(k_ms + EPS) * kn_ref[...]

    c = cos_ref[...]
    s = sin_ref[...]
    scale = LOG2_E * HEAD_DIM ** -0.5
    for j in range(ATTN_WIDTH // 128):
        qt = (_rope(qn[:, 128 * j:128 * (j + 1)], c, s) * scale).T
        qT_ref[0, 2 * j] = qt[:HEAD_DIM].astype(BF16)
        qT_ref[0, 2 * j + 1] = qt[HEAD_DIM:].astype(BF16)
    kr = _rope(kn, c, s).astype(BF16)
    k_ref[0, 0] = kr[:, :HEAD_DIM]
    k_ref[0, 1] = kr[:, HEAD_DIM:]
    vt = v.T.astype(BF16)
    tk = vT_ref.shape[4]
    ones = jnp.ones((V_ROWS - HEAD_DIM, tk), BF16)
    for g in range(N_KV_HEADS):
        for cc in range(vT_ref.shape[2]):
            vT_ref[0, g, cc, :HEAD_DIM] = vt[g * HEAD_DIM:(g + 1) * HEAD_DIM, cc * tk:(cc + 1) * tk]
            vT_ref[0, g, cc, HEAD_DIM:] = ones

    for c0 in range(0, REST_WIDTH, 512):
        zr_ref[0, :, c0:c0 + 512] = jnp.dot(
            h, wrest_ref[:, c0:c0 + 512], preferred_element_type=F32).astype(BF16)


def _proj_call(x, ln_pre, w_qkv, w_rest, bd, qn, kn, cos_t, sin_t, tm, tk):
    B, L, _ = x.shape
    nblk = L // tm
    cpb = tm // tk
    return pl.pallas_call(
        _proj_kernel,
        grid=(B, nblk),
        in_specs=[
            pl.BlockSpec((1, tm, D_MODEL), lambda b, i: (b, i, 0)),
            _const_spec((1, D_MODEL)),
            _const_spec((D_MODEL, QKV_WIDTH)),
            _const_spec((D_MODEL, REST_WIDTH)),
            _const_spec((ATTN_WIDTH, ATTN_WIDTH)),
            _const_spec((1, ATTN_WIDTH)),
            _const_spec((1, KV_WIDTH)),
            pl.BlockSpec((tm, 128), lambda b, i: (i, 0)),
            pl.BlockSpec((tm, 128), lambda b, i: (i, 0)),
        ],
        out_specs=[
            pl.BlockSpec((1, N_HEADS, HEAD_DIM, tm), lambda b, i: (b, 0, 0, i)),
            pl.BlockSpec((1, N_KV_HEADS, tm, HEAD_DIM), lambda b, i: (b, 0, i, 0)),
            pl.BlockSpec((1, N_KV_HEADS, cpb, V_ROWS, tk), lambda b, i: (b, 0, i, 0, 0)),
            pl.BlockSpec((1, tm, REST_WIDTH), lambda b, i: (b, i, 0)),
        ],
        out_shape=[
            jax.ShapeDtypeStruct((B, N_HEADS, HEAD_DIM, L), BF16),
            jax.ShapeDtypeStruct((B, N_KV_HEADS, L, HEAD_DIM), BF16),
            jax.ShapeDtypeStruct((B, N_KV_HEADS, L // tk, V_ROWS, tk), BF16),
            jax.ShapeDtypeStruct((B, L, REST_WIDTH), BF16),
        ],
        compiler_params=pltpu.CompilerParams(
            dimension_semantics=("arbitrary", "arbitrary"), vmem_limit_bytes=VMEM_LIMIT),
        name="proj",
    )(x, ln_pre, w_qkv, w_rest, bd, qn, kn, cos_t, sin_t)


def _mem_kernel(mem_ref, lnmem_ref, wkv_ref, mkT_ref, mv_ref):
    m = mem_ref[0]
    ms = jnp.mean(m * m, axis=-1, keepdims=True)
    mn = (m * lax.rsqrt(ms + EPS) * lnmem_ref[...]).astype(BF16)
    kv = jnp.dot(mn, wkv_ref[...], preferred_element_type=F32)
    mkT_ref[0] = kv[:, :X_WIDTH].T.astype(BF16)
    mv_ref[0] = kv[:, X_WIDTH:].astype(BF16)


def _mem_call(mem, ln_mem, w_mem_kv):
    B, M, _ = mem.shape
    return pl.pallas_call(
        _mem_kernel,
        grid=(B,),
        in_specs=[
            pl.BlockSpec((1, M, D_MODEL), lambda b: (b, 0, 0)),
            _const_spec((1, D_MODEL)),
            _const_spec((D_MODEL, 2 * X_WIDTH)),
        ],
        out_specs=[
            pl.BlockSpec((1, X_WIDTH, M), lambda b: (b, 0, 0)),
            pl.BlockSpec((1, M, X_WIDTH), lambda b: (b, 0, 0)),
        ],
        out_shape=[
            jax.ShapeDtypeStruct((B, X_WIDTH, M), BF16),
            jax.ShapeDtypeStruct((B, M, X_WIDTH), BF16),
        ],
        compiler_params=pltpu.CompilerParams(
            dimension_semantics=("arbitrary",), vmem_limit_bytes=VMEM_LIMIT),
        name="memkv",
    )(mem, ln_mem, w_mem_kv)


def _attn_kernel(qT_ref, k_ref, vT_ref, o_ref, acc_ref, m_ref, s_ref, p_ref, alpha_ref, cmax_ref, lead_ref, *,
                 tk, n_chunks):
    def key_scores(c, hh):
        start = pl.multiple_of(c * tk, tk)
        ks = k_ref[0, 0, pl.ds(start, tk), :]
        return jnp.dot(ks, qT_ref[0, hh], preferred_element_type=F32)

    def pv_product(c, hh):
        return jnp.dot(vT_ref[0, 0, c], p_ref[hh % 2], preferred_element_type=F32)

    def finalize():
        for hp in range(HEADS_PER_KV // 2):
            outs = []
            for hh in (2 * hp, 2 * hp + 1):
                a = acc_ref[hh]
                outs.append(a[:HEAD_DIM] / a[HEAD_DIM:HEAD_DIM + 1])
            o = jnp.concatenate(outs, axis=0).T
            o_ref[0, :, 128 * hp:128 * (hp + 1)] = o.astype(BF16)

    def spec_scores(c, hh, first):
        s = key_scores(c, hh)
        ref = s[0:1, :] if first else m_ref[hh]
        p_ref[hh % 2] = jnp.exp2(s - ref).astype(BF16)
        m_new = jnp.maximum(ref, jnp.max(s, axis=0, keepdims=True))
        alpha_ref[hh % 2] = jnp.exp2(ref - m_new)
        lead_ref[...] = jnp.maximum(lead_ref[...], m_new - ref)
        m_ref[hh] = m_new

    def spec_accumulate(c, hh):
        acc_ref[hh] = (acc_ref[hh] + pv_product(c, hh)) * alpha_ref[hh % 2]

    def spec_chunk(c, first, last):
        for hh in range(HEADS_PER_KV):
            if hh + 1 < HEADS_PER_KV:
                spec_scores(c, hh + 1, first)
            elif not last:
                spec_scores(c + 1, 0, False)
            spec_accumulate(c, hh)

    def spec_loop_body(c, carry):
        spec_chunk(c, first=False, last=False)
        return carry

    acc_ref[...] = jnp.zeros_like(acc_ref)
    lead_ref[...] = jnp.zeros_like(lead_ref)
    spec_scores(0, 0, True)
    spec_chunk(0, first=True, last=False)
    lax.fori_loop(1, n_chunks - 1, spec_loop_body, 0, unroll=CHUNK_UNROLL)
    spec_chunk(n_chunks - 1, first=False, last=True)
    finalize()

    def safe_scores(c, hh):
        s = key_scores(c, hh)
        s_ref[hh % 2] = s
        cmax_ref[hh % 2] = jnp.max(s, axis=0, keepdims=True)

    def safe_softmax(hh):
        m_old = m_ref[hh]
        m_new = jnp.maximum(m_old, cmax_ref[hh % 2])
        p_ref[hh % 2] = jnp.exp2(s_ref[hh % 2] - m_new).astype(BF16)
        alpha_ref[hh % 2] = jnp.exp2(m_old - m_new)
        m_ref[hh] = m_new

    def safe_accumulate(c, hh):
        acc_ref[hh] = acc_ref[hh] * alpha_ref[hh % 2] + pv_product(c, hh)

    def safe_chunk(c, last):
        for hh in range(HEADS_PER_KV):
            if hh + 2 < HEADS_PER_KV:
                safe_scores(c, hh + 2)
            elif not last:
                safe_scores(c + 1, hh + 2 - HEADS_PER_KV)
            if hh + 1 < HEADS_PER_KV:
                safe_softmax(hh + 1)
            elif not last:
                safe_softmax(0)
            safe_accumulate(c, hh)

    def safe_loop_body(c, carry):
        safe_chunk(c, last=False)
        return carry

    @pl.when(jnp.max(lead_ref[...]) > MAX_LEAD)
    def _exact_max_path():
        acc_ref[...] = jnp.zeros_like(acc_ref)
        m_ref[...] = jnp.full_like(m_ref, -jnp.inf)
        safe_scores(0, 0)
        safe_scores(0, 1)
        safe_softmax(0)
        lax.fori_loop(0, n_chunks - 1, safe_loop_body, 0)
        safe_chunk(n_chunks - 1, last=True)
        finalize()


def _attn_call(qT, kk, vT, tq):
    B, _, _, L = qT.shape
    n_chunks, tk = vT.shape[2], vT.shape[4]
    kern = functools.partial(_attn_kernel, tk=tk, n_chunks=n_chunks)
    return pl.pallas_call(
        kern,
        grid=(B, N_KV_HEADS, L // tq),
        in_specs=[
            pl.BlockSpec((1, HEADS_PER_KV, HEAD_DIM, tq), lambda b, g, i: (b, g, 0, i)),
            pl.BlockSpec((1, 1, L, HEAD_DIM), lambda b, g, i: (b, g, 0, 0)),
            pl.BlockSpec((1, 1, n_chunks, V_ROWS, tk), lambda b, g, i: (b, g, 0, 0, 0)),
        ],
        out_specs=pl.BlockSpec((1, tq, HEADS_PER_KV * HEAD_DIM), lambda b, g, i: (b, i, g)),
        out_shape=jax.ShapeDtypeStruct((B, L, ATTN_WIDTH), BF16),
        scratch_shapes=[
            pltpu.VMEM((HEADS_PER_KV, V_ROWS, tq), F32),
            pltpu.VMEM((HEADS_PER_KV, 1, tq), F32),
            pltpu.VMEM((2, tk, tq), F32),
            pltpu.VMEM((2, tk, tq), BF16),
            pltpu.VMEM((2, 1, tq), F32),
            pltpu.VMEM((2, 1, tq), F32),
            pltpu.VMEM((1, tq), F32),
        ],
        compiler_params=pltpu.CompilerParams(
            dimension_semantics=("arbitrary", "arbitrary", "arbitrary"), vmem_limit_bytes=VMEM_LIMIT),
        name="attn",
    )(qT, kk, vT)


def _silu_of_half(hg):
    return hg + hg * jnp.tanh(hg)


def _shift_rows(a, k):
    return pltpu.roll(a, k % a.shape[0], axis=0)


def _out_kernel(x_ref, zr_ref, prev_ref, next_ref, att_ref, mkT_ref, mv_ref, wpool_ref, pscale_ref,
                wbr_ref, bmerge_ref, wout_ref, lnpost_ref, y_ref, *, tm, seq_len):
    i = pl.program_id(1)
    nblk = pl.num_programs(1)

    prev_ok = (i > 0).astype(F32)
    next_ok = (i < nblk - 1).astype(F32)
    u_cur = zr_ref[0, :, R_POOL_IN:R_POOL_IN + POOL_WIDTH].astype(F32)
    u_ext = jnp.concatenate([prev_ref[0].astype(F32) * prev_ok, u_cur, next_ref[0].astype(F32) * next_ok],
                            axis=0)
    t = i * tm + lax.broadcasted_iota(jnp.int32, (tm, POOL_GROUP_DIM), 0)
    pool_gate_h = zr_ref[0, :, R_POOL_GATE:R_POOL_GATE + POOL_WIDTH].astype(F32)
    pooled_parts = []
    for g, w in enumerate(POOL_WINDOWS):
        a = u_ext[:, g * POOL_GROUP_DIM:(g + 1) * POOL_GROUP_DIM]
        ssum = a + _shift_rows(a, 1)
        reach = 1
        while 2 * reach < w:
            ssum = _shift_rows(ssum, reach) + _shift_rows(ssum, -reach)
            reach *= 2
        ssum = ssum[HALO:HALO + tm]
        lo = jnp.maximum(t - w // 2, 0)
        hi = jnp.minimum(t + (w - w // 2), seq_len)
        cnt = (hi - lo).astype(F32)
        mixed = (ssum / cnt - a[HALO:HALO + tm]).astype(BF16)
        pooled_parts.append(jnp.dot(mixed, wpool_ref[g], preferred_element_type=F32))
    pool_out = jnp.concatenate(pooled_parts, axis=1) * pscale_ref[...]
    br_pool = (pool_out * _silu_of_half(pool_gate_h)).astype(BF16)

    attn_gate_h = zr_ref[0, :, R_ATTN_GATE:R_ATTN_GATE + ATTN_WIDTH].astype(F32)
    br_attn = (att_ref[0].astype(F32) * _silu_of_half(attn_gate_h)).astype(BF16)

    cross_parts = []
    for hh in range(N_X_HEADS):
        lo_c = hh * X_HEAD_DIM
        xq = zr_ref[0, :, R_XQ + lo_c:R_XQ + lo_c + X_HEAD_DIM]
        s = jnp.dot(xq, mkT_ref[0, lo_c:lo_c + X_HEAD_DIM, :], preferred_element_type=F32)
        p = jnp.exp2(s - jnp.max(s, axis=-1, keepdims=True))
        den = jnp.sum(p, axis=-1, keepdims=True)
        o = jnp.dot(p.astype(BF16), mv_ref[0, :, lo_c:lo_c + X_HEAD_DIM], preferred_element_type=F32)
        cross_parts.append(o / den)
    x_gate_h = zr_ref[0, :, R_X_GATE:R_X_GATE + X_WIDTH].astype(F32)
    br_cross = (jnp.concatenate(cross_parts, axis=1) * _silu_of_half(x_gate_h)).astype(BF16)

    merged2 = None
    for n, br in enumerate((br_pool, br_attn, br_cross)):
        proj = jnp.dot(br, wbr_ref[n], preferred_element_type=F32)
        logits_h = zr_ref[0, :, R_MERGE + n * D_MODEL:R_MERGE + (n + 1) * D_MODEL].astype(F32)
        gated2 = proj + jnp.tanh(logits_h + 0.5 * bmerge_ref[n:n + 1, :]) * proj
        merged2 = gated2 if merged2 is None else merged2 + gated2

    o = jnp.dot(merged2.astype(BF16), wout_ref[...], preferred_element_type=F32)
    ms = jnp.mean(o * o, axis=-1, keepdims=True)
    y_ref[0] = x_ref[0] + o * lax.rsqrt(ms + EPS) * lnpost_ref[...]


def _out_call(x, zr, att, mkT, mv, w_pool, pool_scale, w_branch, b_merge, w_out, ln_post, tm):
    B, L, _ = x.shape
    M = mv.shape[1]
    hb = tm // HALO
    last_halo = L // HALO - 1
    kern = functools.partial(_out_kernel, tm=tm, seq_len=L)
    return pl.pallas_call(
        kern,
        grid=(B, L // tm),
        in_specs=[
            pl.BlockSpec((1, tm, D_MODEL), lambda b, i: (b, i, 0)),
            pl.BlockSpec((1, tm, REST_WIDTH), lambda b, i: (b, i, 0)),
            pl.BlockSpec((1, HALO, POOL_WIDTH), lambda b, i: (b, jnp.maximum(i * hb - 1, 0), 0)),
            pl.BlockSpec((1, HALO, POOL_WIDTH), lambda b, i: (b, jnp.minimum((i + 1) * hb, last_halo), 0)),
            pl.BlockSpec((1, tm, ATTN_WIDTH), lambda b, i: (b, i, 0)),
            pl.BlockSpec((1, X_WIDTH, M), lambda b, i: (b, 0, 0)),
            pl.BlockSpec((1, M, X_WIDTH), lambda b, i: (b, 0, 0)),
            _const_spec((POOL_GROUPS, POOL_GROUP_DIM, POOL_GROUP_DIM)),
            _const_spec((1, POOL_WIDTH)),
            _const_spec((N_BRANCH, BRANCH_WIDTH, D_MODEL)),
            _const_spec((N_BRANCH, D_MODEL)),
            _const_spec((D_MODEL, D_MODEL)),
            _const_spec((1, D_MODEL)),
        ],
        out_specs=pl.BlockSpec((1, tm, D_MODEL), lambda b, i: (b, i, 0)),
        out_shape=jax.ShapeDtypeStruct((B, L, D_MODEL), F32),
        compiler_params=pltpu.CompilerParams(
            dimension_semantics=("arbitrary", "arbitrary"), vmem_limit_bytes=VMEM_LIMIT),
        name="mixout",
    )(x, zr, zr, zr, att, mkT, mv, w_pool, pool_scale, w_branch, b_merge, w_out, ln_post)


def _rope_tables(L):
    pos = jnp.arange(L, dtype=jnp.int32)
    row = (pos // GRID_W).astype(F32)
    col = (pos % GRID_W).astype(F32)
    axis_dim = HEAD_DIM // 2
    inv = ROPE_THETA ** (-jnp.arange(0, axis_dim, 2, dtype=F32) / axis_dim)
    ang = jnp.concatenate([row[:, None] * inv, col[:, None] * inv], axis=-1)
    cos = jnp.repeat(jnp.cos(ang), 2, axis=-1)
    sin = jnp.repeat(jnp.sin(ang), 2, axis=-1) * jnp.tile(jnp.array([-1.0, 1.0], F32), HEAD_DIM // 2)
    return jnp.tile(cos, (1, 2)), jnp.tile(sin, (1, 2))


def _layer(x, mem, prm, tm, tq, tk):
    B, L, _ = x.shape
    cos_t, sin_t = _rope_tables(L)
    qT, kk, vT, zr = _proj_call(x, prm["ln_pre"], prm["w_qkv"], prm["w_rest"], prm["bd"], prm["qn"], prm["kn"],
                                cos_t, sin_t, tm, tk)
    mkT, mv = _mem_call(mem, prm["ln_mem"], prm["w_mem_kv"])
    att = _attn_call(qT, kk, vT, tq)
    return _out_call(x, zr, att, mkT, mv, prm["w_pool"], prm["pool_scale"], prm["w_branch"], prm["b_merge"],
                     prm["w_out"], prm["ln_post"], tm)


def _prep_params(ln_pre, ln_post, ln_mem, w_in, b_merge, q_norm, k_norm, w_pool, pool_scale, w_mem_kv,
                 w_branch, w_out):
    head_id = np.arange(ATTN_WIDTH) // HEAD_DIM
    bd = (head_id[:, None] == head_id[None, :]).astype(np.float32) / HEAD_DIM
    col_scale = np.ones((REST_WIDTH,), np.float32)
    col_scale[R_POOL_GATE:R_POOL_GATE + POOL_WIDTH] = 0.5
    col_scale[R_ATTN_GATE:R_ATTN_GATE + ATTN_WIDTH] = 0.5
    col_scale[R_XQ:R_XQ + X_WIDTH] = LOG2_E * X_HEAD_DIM ** -0.5
    col_scale[R_X_GATE:R_X_GATE + X_WIDTH] = 0.5
    col_scale[R_MERGE:] = 0.5
    w_rest = jnp.concatenate([w_in[:, :_Q0], w_in[:, _AG0:]], axis=1) * col_scale
    return {
        "ln_pre": ln_pre.reshape(1, D_MODEL),
        "ln_post": ln_post.reshape(1, D_MODEL),
        "ln_mem": ln_mem.reshape(1, D_MODEL),
        "w_qkv": w_in[:, _Q0:_AG0].astype(BF16),
        "w_rest": w_rest.astype(BF16),
        "bd": jnp.asarray(bd, BF16),
        "qn": jnp.tile(q_norm, N_HEADS).reshape(1, ATTN_WIDTH),
        "kn": jnp.tile(k_norm, N_KV_HEADS).reshape(1, KV_WIDTH),
        "w_pool": w_pool.astype(BF16),
        "pool_scale": pool_scale.reshape(1, POOL_WIDTH),
        "w_mem_kv": w_mem_kv.astype(BF16),
        "w_branch": w_branch.astype(BF16),
        "b_merge": b_merge,
        "w_out": (0.5 * w_out).astype(BF16),
    }


def kernel(x_prompt, x_sample, mem_prompt, mem_sample, ln_pre, ln_post, ln_mem, w_in, b_merge, q_norm, k_norm,
           w_pool, pool_scale, w_mem_kv, w_branch, w_out):
    prm = _prep_params(ln_pre, ln_post, ln_mem, w_in, b_merge, q_norm, k_norm, w_pool, pool_scale, w_mem_kv,
                       w_branch, w_out)
    outs = []
    for x, mem in ((x_prompt, mem_prompt), (x_sample, mem_sample)):
        L = x.shape[1]
        outs.append(_layer(x, mem, prm, tm=min(512, L), tq=min(512, L), tk=min(256, L)))
    return tuple(outs)
```

```python
import functools

import numpy as np
import jax
import jax.numpy as jnp
from jax import lax
from jax.experimental import pallas as pl
from jax.experimental.pallas import tpu as pltpu

F32 = jnp.float32
BF16 = jnp.bfloat16

D_MODEL = 1024
GRID_W = 64
EPS = 1e-6
POOL_GROUPS = 4
POOL_GROUP_DIM = 128
POOL_WIDTH = 512
POOL_WINDOWS = (2, 4, 8, 16)
N_HEADS = 8
N_KV_HEADS = 2
HEAD_DIM = 64
HEADS_PER_KV = N_HEADS // N_KV_HEADS
ATTN_WIDTH = 512
KV_WIDTH = 128
ROPE_THETA = 10000.0
LOG2_E = 1.4426950408889634
N_X_HEADS = 4
X_HEAD_DIM = 128
X_WIDTH = 512
N_BRANCH = 3
BRANCH_WIDTH = 512

_Q0 = 2 * POOL_WIDTH
_K0 = _Q0 + ATTN_WIDTH
_V0 = _K0 + KV_WIDTH
_AG0 = _V0 + KV_WIDTH
QKV_WIDTH = ATTN_WIDTH + 2 * KV_WIDTH
R_POOL_IN = 0
R_POOL_GATE = 512
R_ATTN_GATE = 1024
R_XQ = 1536
R_X_GATE = 2048
R_MERGE = 2560
REST_WIDTH = R_MERGE + N_BRANCH * D_MODEL

V_ROWS = HEAD_DIM + 16
HALO = 16
VMEM_LIMIT = 56 * 1024 * 1024
CHUNK_UNROLL = 3
MAX_LEAD = 64.0


def _const_spec(shape):
    n = len(shape)
    return pl.BlockSpec(shape, lambda *_: (0,) * n, pipeline_mode=pl.Buffered(1))


def _rope(t, c, s):
    lane = lax.broadcasted_iota(jnp.int32, t.shape, 1)
    nxt = pltpu.roll(t, t.shape[1] - 1, axis=1)
    prv = pltpu.roll(t, 1, axis=1)
    return t * c + jnp.where(lane % 2 == 0, nxt, prv) * s


def _proj_kernel(x_ref, lnpre_ref, wqkv_ref, wrest_ref, bd_ref, qn_ref, kn_ref, cos_ref, sin_ref, bmerge_ref,
                 qT_ref, k_ref, vT_ref, zr_ref):
    x = x_ref[0]
    ms = jnp.mean(x * x, axis=-1, keepdims=True)
    h = (x * lax.rsqrt(ms + EPS) * lnpre_ref[...]).astype(BF16)

    qkv = jnp.dot(h, wqkv_ref[...], preferred_element_type=F32)
    q = qkv[:, :ATTN_WIDTH]
    k = qkv[:, ATTN_WIDTH:ATTN_WIDTH + KV_WIDTH]
    v = qkv[:, ATTN_WIDTH + KV_WIDTH:]

    bd = bd_ref[...]
    q_ms = jnp.dot((q * q).astype(BF16), bd, preferred_element_type=F32)
    k_ms = jnp.dot((k * k).astype(BF16), bd[:KV_WIDTH, :KV_WIDTH], preferred_element_type=F32)
    qn = q * lax.rsqrt(q_ms + EPS) * qn_ref[...]
    kn = k * lax.rsqrt(k_ms + EPS) * kn_ref[...]

    c = cos_ref[...]
    s = sin_ref[...]
    scale = LOG2_E * HEAD_DIM ** -0.5
    for j in range(ATTN_WIDTH // 128):
        qt = (_rope(qn[:, 128 * j:128 * (j + 1)], c, s) * scale).T
        qT_ref[0, 2 * j] = qt[:HEAD_DIM].astype(BF16)
        qT_ref[0, 2 * j + 1] = qt[HEAD_DIM:].astype(BF16)
    kr = _rope(kn, c, s).astype(BF16)
    k_ref[0, 0] = kr[:, :HEAD_DIM]
    k_ref[0, 1] = kr[:, HEAD_DIM:]
    vt = v.T.astype(BF16)
    tk = vT_ref.shape[4]
    ones = jnp.ones((V_ROWS - HEAD_DIM, tk), BF16)
    for g in range(N_KV_HEADS):
        for cc in range(vT_ref.shape[2]):
            vT_ref[0, g, cc, :HEAD_DIM] = vt[g * HEAD_DIM:(g + 1) * HEAD_DIM, cc * tk:(cc + 1) * tk]
            vT_ref[0, g, cc, HEAD_DIM:] = ones

    for c0 in range(0, REST_WIDTH, 512):
        z = jnp.dot(h, wrest_ref[:, c0:c0 + 512], preferred_element_type=F32)
        if c0 in (R_POOL_GATE, R_ATTN_GATE, R_X_GATE):
            z = z + z * jnp.tanh(z)
        elif c0 >= R_MERGE:
            z = jnp.tanh(z + 0.5 * bmerge_ref[:, c0 - R_MERGE:c0 - R_MERGE + 512])
        zr_ref[0, :, c0:c0 + 512] = z.astype(BF16)


def _proj_call(x, ln_pre, w_qkv, w_rest, bd, qn, kn, cos_t, sin_t, b_merge_row, tm, tk):
    B, L, _ = x.shape
    nblk = L // tm
    cpb = tm // tk
    return pl.pallas_call(
        _proj_kernel,
        grid=(B, nblk),
        in_specs=[
            pl.BlockSpec((1, tm, D_MODEL), lambda b, i: (b, i, 0)),
            _const_spec((1, D_MODEL)),
            _const_spec((D_MODEL, QKV_WIDTH)),
            _const_spec((D_MODEL, REST_WIDTH)),
            _const_spec((ATTN_WIDTH, ATTN_WIDTH)),
            _const_spec((1, ATTN_WIDTH)),
            _const_spec((1, KV_WIDTH)),
            pl.BlockSpec((tm, 128), lambda b, i: (i, 0)),
            pl.BlockSpec((tm, 128), lambda b, i: (i, 0)),
            _const_spec((1, N_BRANCH * D_MODEL)),
        ],
        out_specs=[
            pl.BlockSpec((1, N_HEADS, HEAD_DIM, tm), lambda b, i: (b, 0, 0, i)),
            pl.BlockSpec((1, N_KV_HEADS, tm, HEAD_DIM), lambda b, i: (b, 0, i, 0)),
            pl.BlockSpec((1, N_KV_HEADS, cpb, V_ROWS, tk), lambda b, i: (b, 0, i, 0, 0)),
            pl.BlockSpec((1, tm, REST_WIDTH), lambda b, i: (b, i, 0)),
        ],
        out_shape=[
            jax.ShapeDtypeStruct((B, N_HEADS, HEAD_DIM, L), BF16),
            jax.ShapeDtypeStruct((B, N_KV_HEADS, L, HEAD_DIM), BF16),
            jax.ShapeDtypeStruct((B, N_KV_HEADS, L // tk, V_ROWS, tk), BF16),
            jax.ShapeDtypeStruct((B, L, REST_WIDTH), BF16),
        ],
        compiler_params=pltpu.CompilerParams(
            dimension_semantics=("arbitrary", "arbitrary"), vmem_limit_bytes=VMEM_LIMIT),
        name="proj",
    )(x, ln_pre, w_qkv, w_rest, bd, qn, kn, cos_t, sin_t, b_merge_row)


def _mem_kernel(mem_ref, lnmem_ref, wkv_ref, mkT_ref, mv_ref):
    m = mem_ref[0]
    ms = jnp.mean(m * m, axis=-1, keepdims=True)
    mn = (m * lax.rsqrt(ms + EPS) * lnmem_ref[...]).astype(BF16)
    kv = jnp.dot(mn, wkv_ref[...], preferred_element_type=F32)
    mkT_ref[0] = kv[:, :X_WIDTH].T.astype(BF16)
    mv_ref[0] = kv[:, X_WIDTH:].astype(BF16)


def _mem_call(mem, ln_mem, w_mem_kv):
    B, M, _ = mem.shape
    return pl.pallas_call(
        _mem_kernel,
        grid=(B,),
        in_specs=[
            pl.BlockSpec((1, M, D_MODEL), lambda b: (b, 0, 0)),
            _const_spec((1, D_MODEL)),
            _const_spec((D_MODEL, 2 * X_WIDTH)),
        ],
        out_specs=[
            pl.BlockSpec((1, X_WIDTH, M), lambda b: (b, 0, 0)),
            pl.BlockSpec((1, M, X_WIDTH), lambda b: (b, 0, 0)),
        ],
        out_shape=[
            jax.ShapeDtypeStruct((B, X_WIDTH, M), BF16),
            jax.ShapeDtypeStruct((B, M, X_WIDTH), BF16),
        ],
        compiler_params=pltpu.CompilerParams(
            dimension_semantics=("arbitrary",), vmem_limit_bytes=VMEM_LIMIT),
        name="memkv",
    )(mem, ln_mem, w_mem_kv)


def _attn_kernel(qT_ref, k_ref, vT_ref, o_ref, acc_ref, m_ref, s_ref, p_ref, alpha_ref, cmax_ref, lead_ref, *,
                 tk, n_chunks):
    def key_scores(c, hh):
        start = pl.multiple_of(c * tk, tk)
        ks = k_ref[0, 0, pl.ds(start, tk), :]
        return jnp.dot(ks, qT_ref[0, hh], preferred_element_type=F32)

    def pv_product(c, hh):
        return jnp.dot(vT_ref[0, 0, c], p_ref[hh % 2], preferred_element_type=F32)

    def finalize():
        for hp in range(HEADS_PER_KV // 2):
            outs = []
            for hh in (2 * hp, 2 * hp + 1):
                a = acc_ref[hh]
                outs.append(a[:HEAD_DIM] / a[HEAD_DIM:HEAD_DIM + 1])
            o = jnp.concatenate(outs, axis=0).T
            o_ref[0, :, 128 * hp:128 * (hp + 1)] = o.astype(BF16)

    def spec_scores(c, hh, first):
        s = key_scores(c, hh)
        ref = s[0:1, :] if first else m_ref[hh]
        p_ref[hh % 2] = jnp.exp2((s - ref).astype(BF16))
        m_new = jnp.maximum(ref, jnp.max(s, axis=0, keepdims=True))
        alpha_ref[hh % 2] = jnp.exp2(ref - m_new)
        lead_ref[...] = jnp.maximum(lead_ref[...], m_new - ref)
        m_ref[hh] = m_new

    def spec_accumulate(c, hh):
        acc_ref[hh] = (acc_ref[hh] + pv_product(c, hh)) * alpha_ref[hh % 2]

    def spec_chunk(c, first, last):
        for hh in range(HEADS_PER_KV):
            if hh + 1 < HEADS_PER_KV:
                spec_scores(c, hh + 1, first)
            elif not last:
                spec_scores(c + 1, 0, False)
            spec_accumulate(c, hh)

    def spec_loop_body(c, carry):
        spec_chunk(c, first=False, last=False)
        return carry

    acc_ref[...] = jnp.zeros_like(acc_ref)
    lead_ref[...] = jnp.zeros_like(lead_ref)
    spec_scores(0, 0, True)
    spec_chunk(0, first=True, last=False)
    lax.fori_loop(1, n_chunks - 1, spec_loop_body, 0, unroll=CHUNK_UNROLL)
    spec_chunk(n_chunks - 1, first=False, last=True)
    finalize()

    def safe_scores(c, hh):
        s = key_scores(c, hh)
        s_ref[hh % 2] = s
        cmax_ref[hh % 2] = jnp.max(s, axis=0, keepdims=True)

    def safe_softmax(hh):
        m_old = m_ref[hh]
        m_new = jnp.maximum(m_old, cmax_ref[hh % 2])
        p_ref[hh % 2] = jnp.exp2(s_ref[hh % 2] - m_new).astype(BF16)
        alpha_ref[hh % 2] = jnp.exp2(m_old - m_new)
        m_ref[hh] = m_new

    def safe_accumulate(c, hh):
        acc_ref[hh] = acc_ref[hh] * alpha_ref[hh % 2] + pv_product(c, hh)

    def safe_chunk(c, last):
        for hh in range(HEADS_PER_KV):
            if hh + 2 < HEADS_PER_KV:
                safe_scores(c, hh + 2)
            elif not last:
                safe_scores(c + 1, hh + 2 - HEADS_PER_KV)
            if hh + 1 < HEADS_PER_KV:
                safe_softmax(hh + 1)
            elif not last:
                safe_softmax(0)
            safe_accumulate(c, hh)

    def safe_loop_body(c, carry):
        safe_chunk(c, last=False)
        return carry

    @pl.when(jnp.max(lead_ref[...]) > MAX_LEAD)
    def _exact_max_path():
        acc_ref[...] = jnp.zeros_like(acc_ref)
        m_ref[...] = jnp.full_like(m_ref, -jnp.inf)
        safe_scores(0, 0)
        safe_scores(0, 1)
        safe_softmax(0)
        lax.fori_loop(0, n_chunks - 1, safe_loop_body, 0)
        safe_chunk(n_chunks - 1, last=True)
        finalize()


def _attn_call(qT, kk, vT, tq):
    B, _, _, L = qT.shape
    n_chunks, tk = vT.shape[2], vT.shape[4]
    kern = functools.partial(_attn_kernel, tk=tk, n_chunks=n_chunks)
    return pl.pallas_call(
        kern,
        grid=(B, N_KV_HEADS, L // tq),
        in_specs=[
            pl.BlockSpec((1, HEADS_PER_KV, HEAD_DIM, tq), lambda b, g, i: (b, g, 0, i)),
            pl.BlockSpec((1, 1, L, HEAD_DIM), lambda b, g, i: (b, g, 0, 0)),
            pl.BlockSpec((1, 1, n_chunks, V_ROWS, tk), lambda b, g, i: (b, g, 0, 0, 0)),
        ],
        out_specs=pl.BlockSpec((1, tq, HEADS_PER_KV * HEAD_DIM), lambda b, g, i: (b, i, g)),
        out_shape=jax.ShapeDtypeStruct((B, L, ATTN_WIDTH), BF16),
        scratch_shapes=[
            pltpu.VMEM((HEADS_PER_KV, V_ROWS, tq), F32),
            pltpu.VMEM((HEADS_PER_KV, 1, tq), F32),
            pltpu.VMEM((2, tk, tq), F32),
            pltpu.VMEM((2, tk, tq), BF16),
            pltpu.VMEM((2, 1, tq), F32),
            pltpu.VMEM((2, 1, tq), F32),
            pltpu.VMEM((1, tq), F32),
        ],
        compiler_params=pltpu.CompilerParams(
            dimension_semantics=("arbitrary", "arbitrary", "arbitrary"), vmem_limit_bytes=VMEM_LIMIT),
        name="attn",
    )(qT, kk, vT)


def _shift_rows(a, k):
    return pltpu.roll(a, k % a.shape[0], axis=0)


def _out_kernel(x_ref, zr_ref, prev_ref, next_ref, att_ref, mkT_ref, mv_ref, wpool_ref, pscale_ref,
                wbr_ref, wout_ref, lnpost_ref, y_ref, *, tm, seq_len):
    i = pl.program_id(1)
    nblk = pl.num_programs(1)

    prev_ok = (i > 0).astype(F32)
    next_ok = (i < nblk - 1).astype(F32)
    u_cur = zr_ref[0, :, R_POOL_IN:R_POOL_IN + POOL_WIDTH].astype(F32)
    u_ext = jnp.concatenate([prev_ref[0].astype(F32) * prev_ok, u_cur, next_ref[0].astype(F32) * next_ok],
                            axis=0)
    t = i * tm + lax.broadcasted_iota(jnp.int32, (tm, POOL_GROUP_DIM), 0)
    pool_silu = zr_ref[0, :, R_POOL_GATE:R_POOL_GATE + POOL_WIDTH].astype(F32)
    pooled_parts = []
    for g, w in enumerate(POOL_WINDOWS):
        a = u_ext[:, g * POOL_GROUP_DIM:(g + 1) * POOL_GROUP_DIM]
        ssum = a + _shift_rows(a, 1)
        reach = 1
        while 2 * reach < w:
            ssum = _shift_rows(ssum, reach) + _shift_rows(ssum, -reach)
            reach *= 2
        ssum = ssum[HALO:HALO + tm]
        lo = jnp.maximum(t - w // 2, 0)
        hi = jnp.minimum(t + (w - w // 2), seq_len)
        cnt = (hi - lo).astype(F32)
        mixed = (ssum / cnt - a[HALO:HALO + tm]).astype(BF16)
        pooled_parts.append(jnp.dot(mixed, wpool_ref[g], preferred_element_type=F32))
    pool_out = jnp.concatenate(pooled_parts, axis=1) * pscale_ref[...]
    br_pool = (pool_out * pool_silu).astype(BF16)

    attn_silu = zr_ref[0, :, R_ATTN_GATE:R_ATTN_GATE + ATTN_WIDTH].astype(F32)
    br_attn = (att_ref[0].astype(F32) * attn_silu).astype(BF16)

    cross_parts = []
    for hh in range(N_X_HEADS):
        lo_c = hh * X_HEAD_DIM
        xq = zr_ref[0, :, R_XQ + lo_c:R_XQ + lo_c + X_HEAD_DIM]
        s = jnp.dot(xq, mkT_ref[0, lo_c:lo_c + X_HEAD_DIM, :], preferred_element_type=F32)
        p = jnp.exp2(s - jnp.max(s, axis=-1, keepdims=True))
        den = jnp.sum(p, axis=-1, keepdims=True)
        o = jnp.dot(p.astype(BF16), mv_ref[0, :, lo_c:lo_c + X_HEAD_DIM], preferred_element_type=F32)
        cross_parts.append(o / den)
    x_silu = zr_ref[0, :, R_X_GATE:R_X_GATE + X_WIDTH].astype(F32)
    br_cross = (jnp.concatenate(cross_parts, axis=1) * x_silu).astype(BF16)

    merged2 = None
    for n, br in enumerate((br_pool, br_attn, br_cross)):
        proj = jnp.dot(br, wbr_ref[n], preferred_element_type=F32)
        gate_tanh = zr_ref[0, :, R_MERGE + n * D_MODEL:R_MERGE + (n + 1) * D_MODEL].astype(F32)
        gated2 = proj + gate_tanh * proj
        merged2 = gated2 if merged2 is None else merged2 + gated2

    o = jnp.dot(merged2.astype(BF16), wout_ref[...], preferred_element_type=F32)
    ms = jnp.mean(o * o, axis=-1, keepdims=True)
    y_ref[0] = x_ref[0] + o * lax.rsqrt(ms + EPS) * lnpost_ref[...]


def _out_call(x, zr, att, mkT, mv, w_pool, pool_scale, w_branch, w_out, ln_post, tm):
    B, L, _ = x.shape
    M = mv.shape[1]
    hb = tm // HALO
    last_halo = L // HALO - 1
    kern = functools.partial(_out_kernel, tm=tm, seq_len=L)
    return pl.pallas_call(
        kern,
        grid=(B, L // tm),
        in_specs=[
            pl.BlockSpec((1, tm, D_MODEL), lambda b, i: (b, i, 0)),
            pl.BlockSpec((1, tm, REST_WIDTH), lambda b, i: (b, i, 0)),
            pl.BlockSpec((1, HALO, POOL_WIDTH), lambda b, i: (b, jnp.maximum(i * hb - 1, 0), 0)),
            pl.BlockSpec((1, HALO, POOL_WIDTH), lambda b, i: (b, jnp.minimum((i + 1) * hb, last_halo), 0)),
            pl.BlockSpec((1, tm, ATTN_WIDTH), lambda b, i: (b, i, 0)),
            pl.BlockSpec((1, X_WIDTH, M), lambda b, i: (b, 0, 0)),
            pl.BlockSpec((1, M, X_WIDTH), lambda b, i: (b, 0, 0)),
            _const_spec((POOL_GROUPS, POOL_GROUP_DIM, POOL_GROUP_DIM)),
            _const_spec((1, POOL_WIDTH)),
            _const_spec((N_BRANCH, BRANCH_WIDTH, D_MODEL)),
            _const_spec((D_MODEL, D_MODEL)),
            _const_spec((1, D_MODEL)),
        ],
        out_specs=pl.BlockSpec((1, tm, D_MODEL), lambda b, i: (b, i, 0)),
        out_shape=jax.ShapeDtypeStruct((B, L, D_MODEL), F32),
        compiler_params=pltpu.CompilerParams(
            dimension_semantics=("arbitrary", "arbitrary"), vmem_limit_bytes=VMEM_LIMIT),
        name="mixout",
    )(x, zr, zr, zr, att, mkT, mv, w_pool, pool_scale, w_branch, w_out, ln_post)


def _rope_tables(L):
    pos = jnp.arange(L, dtype=jnp.int32)
    row = (pos // GRID_W).astype(F32)
    col = (pos % GRID_W).astype(F32)
    axis_dim = HEAD_DIM // 2
    inv = ROPE_THETA ** (-jnp.arange(0, axis_dim, 2, dtype=F32) / axis_dim)
    ang = jnp.concatenate([row[:, None] * inv, col[:, None] * inv], axis=-1)
    cos = jnp.repeat(jnp.cos(ang), 2, axis=-1)
    sin = jnp.repeat(jnp.sin(ang), 2, axis=-1) * jnp.tile(jnp.array([-1.0, 1.0], F32), HEAD_DIM // 2)
    return jnp.tile(cos, (1, 2)), jnp.tile(sin, (1, 2))


def _layer(x, mem, prm, cos_t, sin_t, tm, tq, tk):
    qT, kk, vT, zr = _proj_call(x, prm["ln_pre"], prm["w_qkv"], prm["w_rest"], prm["bd"], prm["qn"], prm["kn"],
                                cos_t, sin_t, prm["b_merge_row"], tm, tk)
    mkT, mv = _mem_call(mem, prm["ln_mem"], prm["w_mem_kv"])
    att = _attn_call(qT, kk, vT, tq)
    return _out_call(x, zr, att, mkT, mv, prm["w_pool"], prm["pool_scale"], prm["w_branch"],
                     prm["w_out"], prm["ln_post"], tm)


def _prep_params(ln_pre, ln_post, ln_mem, w_in, b_merge, q_norm, k_norm, w_pool, pool_scale, w_mem_kv,
                 w_branch, w_out):
    head_id = np.arange(ATTN_WIDTH) // HEAD_DIM
    bd = (head_id[:, None] == head_id[None, :]).astype(np.float32) / HEAD_DIM
    col_scale = np.ones((REST_WIDTH,), np.float32)
    col_scale[R_POOL_GATE:R_POOL_GATE + POOL_WIDTH] = 0.5
    col_scale[R_ATTN_GATE:R_ATTN_GATE + ATTN_WIDTH] = 0.5
    col_scale[R_XQ:R_XQ + X_WIDTH] = LOG2_E * X_HEAD_DIM ** -0.5
    col_scale[R_X_GATE:R_X_GATE + X_WIDTH] = 0.5
    col_scale[R_MERGE:] = 0.5
    w_rest = jnp.concatenate([w_in[:, :_Q0], w_in[:, _AG0:]], axis=1) * col_scale
    return {
        "ln_pre": ln_pre.reshape(1, D_MODEL),
        "ln_post": ln_post.reshape(1, D_MODEL),
        "ln_mem": ln_mem.reshape(1, D_MODEL),
        "w_qkv": w_in[:, _Q0:_AG0].astype(BF16),
        "w_rest": w_rest.astype(BF16),
        "bd": jnp.asarray(bd, BF16),
        "qn": jnp.tile(q_norm, N_HEADS).reshape(1, ATTN_WIDTH),
        "kn": jnp.tile(k_norm, N_KV_HEADS).reshape(1, KV_WIDTH),
        "w_pool": w_pool.astype(BF16),
        "pool_scale": pool_scale.reshape(1, POOL_WIDTH),
        "w_mem_kv": w_mem_kv.astype(BF16),
        "w_branch": w_branch.astype(BF16),
        "b_merge_row": b_merge.reshape(1, N_BRANCH * D_MODEL),
        "w_out": (0.5 * w_out).astype(BF16),
    }


def kernel(x_prompt, x_sample, mem_prompt, mem_sample, ln_pre, ln_post, ln_mem, w_in, b_merge, q_norm, k_norm,
           w_pool, pool_scale, w_mem_kv, w_branch, w_out):
    prm = _prep_params(ln_pre, ln_post, ln_mem, w_in, b_merge, q_norm, k_norm, w_pool, pool_scale, w_mem_kv,
                       w_branch, w_out)
    cos_t, sin_t = _rope_tables(max(x_prompt.shape[1], x_sample.shape[1]))
    outs = []
    for x, mem in ((x_prompt, mem_prompt), (x_sample, mem_sample)):
        L = x.shape[1]
        outs.append(_layer(x, mem, prm, cos_t, sin_t, tm=min(512, L), tq=min(512, L), tk=min(256, L)))
    return tuple(outs)
```

```python
import functools

import numpy as np
import jax
import jax.numpy as jnp
from jax import lax
from jax.experimental import pallas as pl
from jax.experimental.pallas import tpu as pltpu

F32 = jnp.float32
BF16 = jnp.bfloat16

D_MODEL = 1024
GRID_W = 64
EPS = 1e-6
POOL_GROUPS = 4
POOL_GROUP_DIM = 128
POOL_WIDTH = 512
POOL_WINDOWS = (2, 4, 8, 16)
N_HEADS = 8
N_KV_HEADS = 2
HEAD_DIM = 64
HEADS_PER_KV = N_HEADS // N_KV_HEADS
ATTN_WIDTH = 512
KV_WIDTH = 128
ROPE_THETA = 10000.0
LOG2_E = 1.4426950408889634
N_X_HEADS = 4
X_HEAD_DIM = 128
X_WIDTH = 512
N_BRANCH = 3
BRANCH_WIDTH = 512

_Q0 = 2 * POOL_WIDTH
_K0 = _Q0 + ATTN_WIDTH
_V0 = _K0 + KV_WIDTH
_AG0 = _V0 + KV_WIDTH
QKV_WIDTH = ATTN_WIDTH + 2 * KV_WIDTH
R_POOL_IN = 0
R_POOL_GATE = 512
R_ATTN_GATE = 1024
R_XQ = 1536
R_X_GATE = 2048
R_MERGE = 2560
REST_WIDTH = R_MERGE + N_BRANCH * D_MODEL

V_ROWS = HEAD_DIM + 16
HALO = 16
VMEM_LIMIT = 56 * 1024 * 1024
CHUNK_UNROLL = 3
ATTN_CHUNKS = 4
MAX_LEAD = 64.0


def _const_spec(shape):
    n = len(shape)
    return pl.BlockSpec(shape, lambda *_: (0,) * n, pipeline_mode=pl.Buffered(1))


def _rope(t, c, s):
    lane = lax.broadcasted_iota(jnp.int32, t.shape, 1)
    nxt = pltpu.roll(t, t.shape[1] - 1, axis=1)
    prv = pltpu.roll(t, 1, axis=1)
    return t * c + jnp.where(lane % 2 == 0, nxt, prv) * s


def _proj_kernel(x_ref, lnpre_ref, wqkv_ref, wrest_ref, bd_ref, qn_ref, kn_ref, cos_ref, sin_ref, bmerge_ref,
                 qT_ref, k_ref, vT_ref, zr_ref):
    x = x_ref[0]
    ms = jnp.mean(x * x, axis=-1, keepdims=True)
    h = (x * lax.rsqrt(ms + EPS) * lnpre_ref[...]).astype(BF16)

    qkv = jnp.dot(h, wqkv_ref[...], preferred_element_type=F32)
    q = qkv[:, :ATTN_WIDTH]
    k = qkv[:, ATTN_WIDTH:ATTN_WIDTH + KV_WIDTH]
    v = qkv[:, ATTN_WIDTH + KV_WIDTH:]

    bd = bd_ref[...]
    q_ms = jnp.dot((q * q).astype(BF16), bd, preferred_element_type=F32)
    k_ms = jnp.dot((k * k).astype(BF16), bd[:KV_WIDTH, :KV_WIDTH], preferred_element_type=F32)
    qn = q * lax.rsqrt(q_ms + EPS) * qn_ref[...]
    kn = k * lax.rsqrt(k_ms + EPS) * kn_ref[...]

    c = cos_ref[...]
    s = sin_ref[...]
    scale = LOG2_E * HEAD_DIM ** -0.5
    for j in range(ATTN_WIDTH // 128):
        qt = (_rope(qn[:, 128 * j:128 * (j + 1)], c, s) * scale).T
        qT_ref[0, 2 * j] = qt[:HEAD_DIM].astype(BF16)
        qT_ref[0, 2 * j + 1] = qt[HEAD_DIM:].astype(BF16)
    kr = _rope(kn, c, s).astype(BF16)
    k_ref[0, 0] = kr[:, :HEAD_DIM]
    k_ref[0, 1] = kr[:, HEAD_DIM:]
    vt = v.T.astype(BF16)
    tk = vT_ref.shape[4]
    ones = jnp.ones((V_ROWS - HEAD_DIM, tk), BF16)
    for g in range(N_KV_HEADS):
        for cc in range(vT_ref.shape[2]):
            vT_ref[0, g, cc, :HEAD_DIM] = vt[g * HEAD_DIM:(g + 1) * HEAD_DIM, cc * tk:(cc + 1) * tk]
            vT_ref[0, g, cc, HEAD_DIM:] = ones

    for c0 in range(0, REST_WIDTH, 512):
        z = jnp.dot(h, wrest_ref[:, c0:c0 + 512], preferred_element_type=F32)
        if c0 in (R_POOL_GATE, R_ATTN_GATE, R_X_GATE):
            z = z + z * jnp.tanh(z)
        elif c0 >= R_MERGE:
            z = jnp.tanh(z + 0.5 * bmerge_ref[:, c0 - R_MERGE:c0 - R_MERGE + 512])
        zr_ref[0, :, c0:c0 + 512] = z.astype(BF16)


def _proj_call(x, ln_pre, w_qkv, w_rest, bd, qn, kn, cos_t, sin_t, b_merge_row, tm, tk):
    B, L, _ = x.shape
    nblk = L // tm
    cpb = tm // tk
    return pl.pallas_call(
        _proj_kernel,
        grid=(B, nblk),
        in_specs=[
            pl.BlockSpec((1, tm, D_MODEL), lambda b, i: (b, i, 0)),
            _const_spec((1, D_MODEL)),
            _const_spec((D_MODEL, QKV_WIDTH)),
            _const_spec((D_MODEL, REST_WIDTH)),
            _const_spec((ATTN_WIDTH, ATTN_WIDTH)),
            _const_spec((1, ATTN_WIDTH)),
            _const_spec((1, KV_WIDTH)),
            pl.BlockSpec((tm, 128), lambda b, i: (i, 0)),
            pl.BlockSpec((tm, 128), lambda b, i: (i, 0)),
            _const_spec((1, N_BRANCH * D_MODEL)),
        ],
        out_specs=[
            pl.BlockSpec((1, N_HEADS, HEAD_DIM, tm), lambda b, i: (b, 0, 0, i)),
            pl.BlockSpec((1, N_KV_HEADS, tm, HEAD_DIM), lambda b, i: (b, 0, i, 0)),
            pl.BlockSpec((1, N_KV_HEADS, cpb, V_ROWS, tk), lambda b, i: (b, 0, i, 0, 0)),
            pl.BlockSpec((1, tm, REST_WIDTH), lambda b, i: (b, i, 0)),
        ],
        out_shape=[
            jax.ShapeDtypeStruct((B, N_HEADS, HEAD_DIM, L), BF16),
            jax.ShapeDtypeStruct((B, N_KV_HEADS, L, HEAD_DIM), BF16),
            jax.ShapeDtypeStruct((B, N_KV_HEADS, L // tk, V_ROWS, tk), BF16),
            jax.ShapeDtypeStruct((B, L, REST_WIDTH), BF16),
        ],
        compiler_params=pltpu.CompilerParams(
            dimension_semantics=("arbitrary", "arbitrary"), vmem_limit_bytes=VMEM_LIMIT),
        name="proj",
    )(x, ln_pre, w_qkv, w_rest, bd, qn, kn, cos_t, sin_t, b_merge_row)


def _mem_kernel(mem_ref, lnmem_ref, wkv_ref, mkT_ref, mv_ref):
    m = mem_ref[0]
    ms = jnp.mean(m * m, axis=-1, keepdims=True)
    mn = (m * lax.rsqrt(ms + EPS) * lnmem_ref[...]).astype(BF16)
    kv = jnp.dot(mn, wkv_ref[...], preferred_element_type=F32)
    mkT_ref[0] = kv[:, :X_WIDTH].T.astype(BF16)
    mv_ref[0] = kv[:, X_WIDTH:].astype(BF16)


def _mem_call(mem, ln_mem, w_mem_kv):
    B, M, _ = mem.shape
    return pl.pallas_call(
        _mem_kernel,
        grid=(B,),
        in_specs=[
            pl.BlockSpec((1, M, D_MODEL), lambda b: (b, 0, 0)),
            _const_spec((1, D_MODEL)),
            _const_spec((D_MODEL, 2 * X_WIDTH)),
        ],
        out_specs=[
            pl.BlockSpec((1, X_WIDTH, M), lambda b: (b, 0, 0)),
            pl.BlockSpec((1, M, X_WIDTH), lambda b: (b, 0, 0)),
        ],
        out_shape=[
            jax.ShapeDtypeStruct((B, X_WIDTH, M), BF16),
            jax.ShapeDtypeStruct((B, M, X_WIDTH), BF16),
        ],
        compiler_params=pltpu.CompilerParams(
            dimension_semantics=("arbitrary",), vmem_limit_bytes=VMEM_LIMIT),
        name="memkv",
    )(mem, ln_mem, w_mem_kv)


def _attn_kernel(qT_ref, k_ref, vT_ref, o_ref, acc_ref, m_ref, s_ref, p_ref, alpha_ref, cmax_ref, lead_ref, *,
                 tk, tv, n_chunks):
    def key_scores(c, hh):
        start = pl.multiple_of(c * tk, tk)
        ks = k_ref[0, 0, pl.ds(start, tk), :]
        return jnp.dot(ks, qT_ref[0, hh], preferred_element_type=F32)

    def pv_product(c, hh):
        pv = None
        for u in range(tk // tv):
            part = jnp.dot(vT_ref[0, 0, c * (tk // tv) + u], p_ref[hh % 2, u * tv:(u + 1) * tv, :],
                           preferred_element_type=F32)
            pv = part if pv is None else pv + part
        return pv

    def finalize():
        for hp in range(HEADS_PER_KV // 2):
            outs = []
            for hh in (2 * hp, 2 * hp + 1):
                a = acc_ref[hh]
                outs.append(a[:HEAD_DIM] / a[HEAD_DIM:HEAD_DIM + 1])
            o = jnp.concatenate(outs, axis=0).T
            o_ref[0, :, 128 * hp:128 * (hp + 1)] = o.astype(BF16)

    def spec_scores(c, hh, first):
        s = key_scores(c, hh)
        ref = s[0:1, :] if first else m_ref[hh]
        p_ref[hh % 2] = jnp.exp2((s - ref).astype(BF16))
        m_new = jnp.maximum(ref, jnp.max(s, axis=0, keepdims=True))
        alpha_ref[hh % 2] = jnp.exp2(ref - m_new)
        lead_ref[...] = jnp.maximum(lead_ref[...], m_new - ref)
        m_ref[hh] = m_new

    def spec_accumulate(c, hh):
        acc_ref[hh] = (acc_ref[hh] + pv_product(c, hh)) * alpha_ref[hh % 2]

    def spec_chunk(c, first, last):
        for hh in range(HEADS_PER_KV):
            if hh + 1 < HEADS_PER_KV:
                spec_scores(c, hh + 1, first)
            elif not last:
                spec_scores(c + 1, 0, False)
            spec_accumulate(c, hh)

    def spec_loop_body(c, carry):
        spec_chunk(c, first=False, last=False)
        return carry

    acc_ref[...] = jnp.zeros_like(acc_ref)
    lead_ref[...] = jnp.zeros_like(lead_ref)
    spec_scores(0, 0, True)
    spec_chunk(0, first=True, last=False)
    lax.fori_loop(1, n_chunks - 1, spec_loop_body, 0, unroll=CHUNK_UNROLL)
    spec_chunk(n_chunks - 1, first=False, last=True)
    finalize()

    def safe_scores(c, hh):
        s = key_scores(c, hh)
        s_ref[hh % 2] = s
        cmax_ref[hh % 2] = jnp.max(s, axis=0, keepdims=True)

    def safe_softmax(hh):
        m_old = m_ref[hh]
        m_new = jnp.maximum(m_old, cmax_ref[hh % 2])
        p_ref[hh % 2] = jnp.exp2(s_ref[hh % 2] - m_new).astype(BF16)
        alpha_ref[hh % 2] = jnp.exp2(m_old - m_new)
        m_ref[hh] = m_new

    def safe_accumulate(c, hh):
        acc_ref[hh] = acc_ref[hh] * alpha_ref[hh % 2] + pv_product(c, hh)

    def safe_chunk(c, last):
        for hh in range(HEADS_PER_KV):
            if hh + 2 < HEADS_PER_KV:
                safe_scores(c, hh + 2)
            elif not last:
                safe_scores(c + 1, hh + 2 - HEADS_PER_KV)
            if hh + 1 < HEADS_PER_KV:
                safe_softmax(hh + 1)
            elif not last:
                safe_softmax(0)
            safe_accumulate(c, hh)

    def safe_loop_body(c, carry):
        safe_chunk(c, last=False)
        return carry

    @pl.when(jnp.max(lead_ref[...]) > MAX_LEAD)
    def _exact_max_path():
        acc_ref[...] = jnp.zeros_like(acc_ref)
        m_ref[...] = jnp.full_like(m_ref, -jnp.inf)
        safe_scores(0, 0)
        safe_scores(0, 1)
        safe_softmax(0)
        lax.fori_loop(0, n_chunks - 1, safe_loop_body, 0)
        safe_chunk(n_chunks - 1, last=True)
        finalize()


def _attn_call(qT, kk, vT, tq, tk):
    B, _, _, L = qT.shape
    n_pieces, tv = vT.shape[2], vT.shape[4]
    n_chunks = L // tk
    kern = functools.partial(_attn_kernel, tk=tk, tv=tv, n_chunks=n_chunks)
    return pl.pallas_call(
        kern,
        grid=(B, N_KV_HEADS, L // tq),
        in_specs=[
            pl.BlockSpec((1, HEADS_PER_KV, HEAD_DIM, tq), lambda b, g, i: (b, g, 0, i)),
            pl.BlockSpec((1, 1, L, HEAD_DIM), lambda b, g, i: (b, g, 0, 0)),
            pl.BlockSpec((1, 1, n_pieces, V_ROWS, tv), lambda b, g, i: (b, g, 0, 0, 0)),
        ],
        out_specs=pl.BlockSpec((1, tq, HEADS_PER_KV * HEAD_DIM), lambda b, g, i: (b, i, g)),
        out_shape=jax.ShapeDtypeStruct((B, L, ATTN_WIDTH), BF16),
        scratch_shapes=[
            pltpu.VMEM((HEADS_PER_KV, V_ROWS, tq), F32),
            pltpu.VMEM((HEADS_PER_KV, 1, tq), F32),
            pltpu.VMEM((2, tk, tq), F32),
            pltpu.VMEM((2, tk, tq), BF16),
            pltpu.VMEM((2, 1, tq), F32),
            pltpu.VMEM((2, 1, tq), F32),
            pltpu.VMEM((1, tq), F32),
        ],
        compiler_params=pltpu.CompilerParams(
            dimension_semantics=("arbitrary", "arbitrary", "arbitrary"), vmem_limit_bytes=VMEM_LIMIT),
        name="attn",
    )(qT, kk, vT)


def _shift_rows(a, k):
    return pltpu.roll(a, k % a.shape[0], axis=0)


def _out_kernel(x_ref, zr_ref, prev_ref, next_ref, att_ref, mkT_ref, mv_ref, wpool_ref, pscale_ref,
                wbr_ref, wout_ref, lnpost_ref, y_ref, *, tm, seq_len):
    i = pl.program_id(1)
    nblk = pl.num_programs(1)

    prev_ok = (i > 0).astype(F32)
    next_ok = (i < nblk - 1).astype(F32)
    u_cur = zr_ref[0, :, R_POOL_IN:R_POOL_IN + POOL_WIDTH].astype(F32)
    u_ext = jnp.concatenate([prev_ref[0].astype(F32) * prev_ok, u_cur, next_ref[0].astype(F32) * next_ok],
                            axis=0)
    t = i * tm + lax.broadcasted_iota(jnp.int32, (tm, POOL_GROUP_DIM), 0)
    pool_silu = zr_ref[0, :, R_POOL_GATE:R_POOL_GATE + POOL_WIDTH].astype(F32)
    pooled_parts = []
    for g, w in enumerate(POOL_WINDOWS):
        a = u_ext[:, g * POOL_GROUP_DIM:(g + 1) * POOL_GROUP_DIM]
        ssum = a + _shift_rows(a, 1)
        reach = 1
        while 2 * reach < w:
            ssum = _shift_rows(ssum, reach) + _shift_rows(ssum, -reach)
            reach *= 2
        ssum = ssum[HALO:HALO + tm]
        lo = jnp.maximum(t - w // 2, 0)
        hi = jnp.minimum(t + (w - w // 2), seq_len)
        cnt = (hi - lo).astype(F32)
        mixed = (ssum / cnt - a[HALO:HALO + tm]).astype(BF16)
        pooled_parts.append(jnp.dot(mixed, wpool_ref[g], preferred_element_type=F32))
    pool_out = jnp.concatenate(pooled_parts, axis=1) * pscale_ref[...]
    br_pool = (pool_out * pool_silu).astype(BF16)

    attn_silu = zr_ref[0, :, R_ATTN_GATE:R_ATTN_GATE + ATTN_WIDTH].astype(F32)
    br_attn = (att_ref[0].astype(F32) * attn_silu).astype(BF16)

    cross_parts = []
    for hh in range(N_X_HEADS):
        lo_c = hh * X_HEAD_DIM
        xq = zr_ref[0, :, R_XQ + lo_c:R_XQ + lo_c + X_HEAD_DIM]
        s = jnp.dot(xq, mkT_ref[0, lo_c:lo_c + X_HEAD_DIM, :], preferred_element_type=F32)
        p = jnp.exp2(s - jnp.max(s, axis=-1, keepdims=True))
        den = jnp.sum(p, axis=-1, keepdims=True)
        o = jnp.dot(p.astype(BF16), mv_ref[0, :, lo_c:lo_c + X_HEAD_DIM], preferred_element_type=F32)
        cross_parts.append(o / den)
    x_silu = zr_ref[0, :, R_X_GATE:R_X_GATE + X_WIDTH].astype(F32)
    br_cross = (jnp.concatenate(cross_parts, axis=1) * x_silu).astype(BF16)

    merged2 = None
    for n, br in enumerate((br_pool, br_attn, br_cross)):
        proj = jnp.dot(br, wbr_ref[n], preferred_element_type=F32)
        gate_tanh = zr_ref[0, :, R_MERGE + n * D_MODEL:R_MERGE + (n + 1) * D_MODEL].astype(F32)
        gated2 = proj + gate_tanh * proj
        merged2 = gated2 if merged2 is None else merged2 + gated2

    o = jnp.dot(merged2.astype(BF16), wout_ref[...], preferred_element_type=F32)
    ms = jnp.mean(o * o, axis=-1, keepdims=True)
    y_ref[0] = x_ref[0] + o * lax.rsqrt(ms + EPS) * lnpost_ref[...]


def _out_call(x, zr, att, mkT, mv, w_pool, pool_scale, w_branch, w_out, ln_post, tm):
    B, L, _ = x.shape
    M = mv.shape[1]
    hb = tm // HALO
    last_halo = L // HALO - 1
    kern = functools.partial(_out_kernel, tm=tm, seq_len=L)
    return pl.pallas_call(
        kern,
        grid=(B, L // tm),
        in_specs=[
            pl.BlockSpec((1, tm, D_MODEL), lambda b, i: (b, i, 0)),
            pl.BlockSpec((1, tm, REST_WIDTH), lambda b, i: (b, i, 0)),
            pl.BlockSpec((1, HALO, POOL_WIDTH), lambda b, i: (b, jnp.maximum(i * hb - 1, 0), 0)),
            pl.BlockSpec((1, HALO, POOL_WIDTH), lambda b, i: (b, jnp.minimum((i + 1) * hb, last_halo), 0)),
            pl.BlockSpec((1, tm, ATTN_WIDTH), lambda b, i: (b, i, 0)),
            pl.BlockSpec((1, X_WIDTH, M), lambda b, i: (b, 0, 0)),
            pl.BlockSpec((1, M, X_WIDTH), lambda b, i: (b, 0, 0)),
            _const_spec((POOL_GROUPS, POOL_GROUP_DIM, POOL_GROUP_DIM)),
            _const_spec((1, POOL_WIDTH)),
            _const_spec((N_BRANCH, BRANCH_WIDTH, D_MODEL)),
            _const_spec((D_MODEL, D_MODEL)),
            _const_spec((1, D_MODEL)),
        ],
        out_specs=pl.BlockSpec((1, tm, D_MODEL), lambda b, i: (b, i, 0)),
        out_shape=jax.ShapeDtypeStruct((B, L, D_MODEL), F32),
        compiler_params=pltpu.CompilerParams(
            dimension_semantics=("arbitrary", "arbitrary"), vmem_limit_bytes=VMEM_LIMIT),
        name="mixout",
    )(x, zr, zr, zr, att, mkT, mv, w_pool, pool_scale, w_branch, w_out, ln_post)


def _rope_tables(L):
    pos = jnp.arange(L, dtype=jnp.int32)
    row = (pos // GRID_W).astype(F32)
    col = (pos % GRID_W).astype(F32)
    axis_dim = HEAD_DIM // 2
    inv = ROPE_THETA ** (-jnp.arange(0, axis_dim, 2, dtype=F32) / axis_dim)
    ang = jnp.concatenate([row[:, None] * inv, col[:, None] * inv], axis=-1)
    cos = jnp.repeat(jnp.cos(ang), 2, axis=-1)
    sin = jnp.repeat(jnp.sin(ang), 2, axis=-1) * jnp.tile(jnp.array([-1.0, 1.0], F32), HEAD_DIM // 2)
    return jnp.tile(cos, (1, 2)), jnp.tile(sin, (1, 2))


def _layer(x, mem, prm, cos_t, sin_t, tm, tq, tk):
    qT, kk, vT, zr = _proj_call(x, prm["ln_pre"], prm["w_qkv"], prm["w_rest"], prm["bd"], prm["qn"], prm["kn"],
                                cos_t, sin_t, prm["b_merge_row"], tm, min(tk, tm))
    mkT, mv = _mem_call(mem, prm["ln_mem"], prm["w_mem_kv"])
    att = _attn_call(qT, kk, vT, tq, tk)
    return _out_call(x, zr, att, mkT, mv, prm["w_pool"], prm["pool_scale"], prm["w_branch"],
                     prm["w_out"], prm["ln_post"], tm)


def _prep_params(ln_pre, ln_post, ln_mem, w_in, b_merge, q_norm, k_norm, w_pool, pool_scale, w_mem_kv,
                 w_branch, w_out):
    head_id = np.arange(ATTN_WIDTH) // HEAD_DIM
    bd = (head_id[:, None] == head_id[None, :]).astype(np.float32) / HEAD_DIM
    col_scale = np.ones((REST_WIDTH,), np.float32)
    col_scale[R_POOL_GATE:R_POOL_GATE + POOL_WIDTH] = 0.5
    col_scale[R_ATTN_GATE:R_ATTN_GATE + ATTN_WIDTH] = 0.5
    col_scale[R_XQ:R_XQ + X_WIDTH] = LOG2_E * X_HEAD_DIM ** -0.5
    col_scale[R_X_GATE:R_X_GATE + X_WIDTH] = 0.5
    col_scale[R_MERGE:] = 0.5
    w_rest = jnp.concatenate([w_in[:, :_Q0], w_in[:, _AG0:]], axis=1) * col_scale
    return {
        "ln_pre": ln_pre.reshape(1, D_MODEL),
        "ln_post": ln_post.reshape(1, D_MODEL),
        "ln_mem": ln_mem.reshape(1, D_MODEL),
        "w_qkv": w_in[:, _Q0:_AG0].astype(BF16),
        "w_rest": w_rest.astype(BF16),
        "bd": jnp.asarray(bd, BF16),
        "qn": jnp.tile(q_norm, N_HEADS).reshape(1, ATTN_WIDTH),
        "kn": jnp.tile(k_norm, N_KV_HEADS).reshape(1, KV_WIDTH),
        "w_pool": w_pool.astype(BF16),
        "pool_scale": pool_scale.reshape(1, POOL_WIDTH),
        "w_mem_kv": w_mem_kv.astype(BF16),
        "w_branch": w_branch.astype(BF16),
        "b_merge_row": b_merge.reshape(1, N_BRANCH * D_MODEL),
        "w_out": (0.5 * w_out).astype(BF16),
    }


def kernel(x_prompt, x_sample, mem_prompt, mem_sample, ln_pre, ln_post, ln_mem, w_in, b_merge, q_norm, k_norm,
           w_pool, pool_scale, w_mem_kv, w_branch, w_out):
    prm = _prep_params(ln_pre, ln_post, ln_mem, w_in, b_merge, q_norm, k_norm, w_pool, pool_scale, w_mem_kv,
                       w_branch, w_out)
    cos_t, sin_t = _rope_tables(max(x_prompt.shape[1], x_sample.shape[1]))
    outs = []
    for x, mem in ((x_prompt, mem_prompt), (x_sample, mem_sample)):
        L = x.shape[1]
        tm = min(512, L)
        tk = max(tm, L // ATTN_CHUNKS)
        outs.append(_layer(x, mem, prm, cos_t, sin_t, tm=tm, tq=min(512, L), tk=tk))
    return tuple(outs)
```

```python
import functools

import numpy as np
import jax
import jax.numpy as jnp
from jax import lax
from jax.experimental import pallas as pl
from jax.experimental.pallas import tpu as pltpu

F32 = jnp.float32
BF16 = jnp.bfloat16

D_MODEL = 1024
GRID_W = 64
EPS = 1e-6
POOL_GROUPS = 4
POOL_GROUP_DIM = 128
POOL_WIDTH = 512
POOL_WINDOWS = (2, 4, 8, 16)
N_HEADS = 8
N_KV_HEADS = 2
HEAD_DIM = 64
HEADS_PER_KV = N_HEADS // N_KV_HEADS
ATTN_WIDTH = 512
KV_WIDTH = 128
ROPE_THETA = 10000.0
LOG2_E = 1.4426950408889634
N_X_HEADS = 4
X_HEAD_DIM = 128
X_WIDTH = 512
N_BRANCH = 3
BRANCH_WIDTH = 512

_Q0 = 2 * POOL_WIDTH
_K0 = _Q0 + ATTN_WIDTH
_V0 = _K0 + KV_WIDTH
_AG0 = _V0 + KV_WIDTH
QKV_WIDTH = ATTN_WIDTH + 2 * KV_WIDTH
R_BR_POOL = 0
R_ATTN_GATE = 512
R_XQ = 1024
R_X_GATE = 1536
R_MERGE = 2048
REST_WIDTH = R_MERGE + N_BRANCH * D_MODEL
PG_WIDTH = 2 * POOL_WIDTH
W_REST_WIDTH = REST_WIDTH - POOL_WIDTH

V_ROWS = HEAD_DIM + 16
HALO = 16
VMEM_LIMIT = 56 * 1024 * 1024
CHUNK_UNROLL = 3
ATTN_CHUNKS = 4
MAX_LEAD = 64.0


def _const_spec(shape):
    n = len(shape)
    return pl.BlockSpec(shape, lambda *_: (0,) * n, pipeline_mode=pl.Buffered(1))


def _rope(t, c, s):
    lane = lax.broadcasted_iota(jnp.int32, t.shape, 1)
    nxt = pltpu.roll(t, t.shape[1] - 1, axis=1)
    prv = pltpu.roll(t, 1, axis=1)
    return t * c + jnp.where(lane % 2 == 0, nxt, prv) * s


def _shift_rows(a, k):
    return pltpu.roll(a, k % a.shape[0], axis=0)


def _proj_kernel(x_ref, xprev_ref, xnext_ref, lnpre_ref, wqkv_ref, wpg_ref, wrest_ref, bd_ref, qn_ref, kn_ref,
                 cos_ref, sin_ref, bmerge_ref, wpool_ref, pscale_ref,
                 qT_ref, k_ref, vT_ref, zr_ref, *, tm, seq_len):
    i = pl.program_id(1)
    nblk = pl.num_programs(1)

    def pre_norm(xv):
        ms = jnp.mean(xv * xv, axis=-1, keepdims=True)
        return (xv * lax.rsqrt(ms + EPS) * lnpre_ref[...]).astype(BF16)

    h = pre_norm(x_ref[0])

    qkv = jnp.dot(h, wqkv_ref[...], preferred_element_type=F32)
    q = qkv[:, :ATTN_WIDTH]
    k = qkv[:, ATTN_WIDTH:ATTN_WIDTH + KV_WIDTH]
    v = qkv[:, ATTN_WIDTH + KV_WIDTH:]

    bd = bd_ref[...]
    q_ms = jnp.dot((q * q).astype(BF16), bd, preferred_element_type=F32)
    k_ms = jnp.dot((k * k).astype(BF16), bd[:KV_WIDTH, :KV_WIDTH], preferred_element_type=F32)
    qn = q * lax.rsqrt(q_ms + EPS) * qn_ref[...]
    kn = k * lax.rsqrt(k_ms + EPS) * kn_ref[...]

    c = cos_ref[...]
    s = sin_ref[...]
    scale = LOG2_E * HEAD_DIM ** -0.5
    for j in range(ATTN_WIDTH // 128):
        qt = (_rope(qn[:, 128 * j:128 * (j + 1)], c, s) * scale).T
        qT_ref[0, 2 * j] = qt[:HEAD_DIM].astype(BF16)
        qT_ref[0, 2 * j + 1] = qt[HEAD_DIM:].astype(BF16)
    kr = _rope(kn, c, s).astype(BF16)
    k_ref[0, 0] = kr[:, :HEAD_DIM]
    k_ref[0, 1] = kr[:, HEAD_DIM:]
    vt = v.T.astype(BF16)
    tk = vT_ref.shape[4]
    ones = jnp.ones((V_ROWS - HEAD_DIM, tk), BF16)
    for g in range(N_KV_HEADS):
        for cc in range(vT_ref.shape[2]):
            vT_ref[0, g, cc, :HEAD_DIM] = vt[g * HEAD_DIM:(g + 1) * HEAD_DIM, cc * tk:(cc + 1) * tk]
            vT_ref[0, g, cc, HEAD_DIM:] = ones

    h_ext = jnp.concatenate([pre_norm(xprev_ref[0]), h, pre_norm(xnext_ref[0])], axis=0)
    u_ext = jnp.dot(h_ext, wpg_ref[:, :POOL_WIDTH], preferred_element_type=F32)
    prev_ok = (i > 0).astype(F32)
    next_ok = (i < nblk - 1).astype(F32)
    u_ext = jnp.concatenate([u_ext[:HALO] * prev_ok, u_ext[HALO:HALO + tm], u_ext[HALO + tm:] * next_ok], axis=0)
    t = i * tm + lax.broadcasted_iota(jnp.int32, (tm, POOL_GROUP_DIM), 0)

    def pooled_group(g):
        w = POOL_WINDOWS[g]
        a = u_ext[:, g * POOL_GROUP_DIM:(g + 1) * POOL_GROUP_DIM]
        ssum = a + _shift_rows(a, 1)
        reach = 1
        while 2 * reach < w:
            ssum = _shift_rows(ssum, reach) + _shift_rows(ssum, -reach)
            reach *= 2
        ssum = ssum[HALO:HALO + tm]
        lo = jnp.maximum(t - w // 2, 0)
        hi = jnp.minimum(t + (w - w // 2), seq_len)
        cnt = (hi - lo).astype(F32)
        mixed = (ssum / cnt - a[HALO:HALO + tm]).astype(BF16)
        return jnp.dot(mixed, wpool_ref[g], preferred_element_type=F32)

    pooled_parts = []
    for idx, c0 in enumerate(range(0, W_REST_WIDTH, 512)):
        z = jnp.dot(h, wrest_ref[:, c0:c0 + 512], preferred_element_type=F32)
        if idx % 2 == 0 and idx // 2 < POOL_GROUPS:
            pooled_parts.append(pooled_group(idx // 2))
        col = POOL_WIDTH + c0
        if col in (R_ATTN_GATE, R_X_GATE):
            z = z + z * jnp.tanh(z)
        elif col >= R_MERGE:
            z = jnp.tanh(z + 0.5 * bmerge_ref[:, col - R_MERGE:col - R_MERGE + 512])
        zr_ref[0, :, col:col + 512] = z.astype(BF16)

    pool_gate_h = jnp.dot(h, wpg_ref[:, POOL_WIDTH:], preferred_element_type=F32)
    pool_silu = pool_gate_h + pool_gate_h * jnp.tanh(pool_gate_h)
    br_pool = jnp.concatenate(pooled_parts, axis=1) * pscale_ref[...] * pool_silu
    zr_ref[0, :, R_BR_POOL:R_BR_POOL + POOL_WIDTH] = br_pool.astype(BF16)


def _proj_call(x, ln_pre, w_qkv, w_pg, w_rest, bd, qn, kn, cos_t, sin_t, b_merge_row, w_pool, pool_scale, tm, tk):
    B, L, _ = x.shape
    nblk = L // tm
    cpb = tm // tk
    hb = tm // HALO
    last_halo = L // HALO - 1
    kern = functools.partial(_proj_kernel, tm=tm, seq_len=L)
    return pl.pallas_call(
        kern,
        grid=(B, nblk),
        in_specs=[
            pl.BlockSpec((1, tm, D_MODEL), lambda b, i: (b, i, 0)),
            pl.BlockSpec((1, HALO, D_MODEL), lambda b, i: (b, jnp.maximum(i * hb - 1, 0), 0)),
            pl.BlockSpec((1, HALO, D_MODEL), lambda b, i: (b, jnp.minimum((i + 1) * hb, last_halo), 0)),
            _const_spec((1, D_MODEL)),
            _const_spec((D_MODEL, QKV_WIDTH)),
            _const_spec((D_MODEL, PG_WIDTH)),
            _const_spec((D_MODEL, W_REST_WIDTH)),
            _const_spec((ATTN_WIDTH, ATTN_WIDTH)),
            _const_spec((1, ATTN_WIDTH)),
            _const_spec((1, KV_WIDTH)),
            pl.BlockSpec((tm, 128), lambda b, i: (i, 0)),
            pl.BlockSpec((tm, 128), lambda b, i: (i, 0)),
            _const_spec((1, N_BRANCH * D_MODEL)),
            _const_spec((POOL_GROUPS, POOL_GROUP_DIM, POOL_GROUP_DIM)),
            _const_spec((1, POOL_WIDTH)),
        ],
        out_specs=[
            pl.BlockSpec((1, N_HEADS, HEAD_DIM, tm), lambda b, i: (b, 0, 0, i)),
            pl.BlockSpec((1, N_KV_HEADS, tm, HEAD_DIM), lambda b, i: (b, 0, i, 0)),
            pl.BlockSpec((1, N_KV_HEADS, cpb, V_ROWS, tk), lambda b, i: (b, 0, i, 0, 0)),
            pl.BlockSpec((1, tm, REST_WIDTH), lambda b, i: (b, i, 0)),
        ],
        out_shape=[
            jax.ShapeDtypeStruct((B, N_HEADS, HEAD_DIM, L), BF16),
            jax.ShapeDtypeStruct((B, N_KV_HEADS, L, HEAD_DIM), BF16),
            jax.ShapeDtypeStruct((B, N_KV_HEADS, L // tk, V_ROWS, tk), BF16),
            jax.ShapeDtypeStruct((B, L, REST_WIDTH), BF16),
        ],
        compiler_params=pltpu.CompilerParams(
            dimension_semantics=("arbitrary", "arbitrary"), vmem_limit_bytes=VMEM_LIMIT),
        name="proj",
    )(x, x, x, ln_pre, w_qkv, w_pg, w_rest, bd, qn, kn, cos_t, sin_t, b_merge_row, w_pool, pool_scale)


def _mem_kernel(mem_ref, lnmem_ref, wkv_ref, mkT_ref, mv_ref):
    m = mem_ref[0]
    ms = jnp.mean(m * m, axis=-1, keepdims=True)
    mn = (m * lax.rsqrt(ms + EPS) * lnmem_ref[...]).astype(BF16)
    kv = jnp.dot(mn, wkv_ref[...], preferred_element_type=F32)
    mkT_ref[0] = kv[:, :X_WIDTH].T.astype(BF16)
    mv_ref[0] = kv[:, X_WIDTH:].astype(BF16)


def _mem_call(mem, ln_mem, w_mem_kv):
    B, M, _ = mem.shape
    return pl.pallas_call(
        _mem_kernel,
        grid=(B,),
        in_specs=[
            pl.BlockSpec((1, M, D_MODEL), lambda b: (b, 0, 0)),
            _const_spec((1, D_MODEL)),
            _const_spec((D_MODEL, 2 * X_WIDTH)),
        ],
        out_specs=[
            pl.BlockSpec((1, X_WIDTH, M), lambda b: (b, 0, 0)),
            pl.BlockSpec((1, M, X_WIDTH), lambda b: (b, 0, 0)),
        ],
        out_shape=[
            jax.ShapeDtypeStruct((B, X_WIDTH, M), BF16),
            jax.ShapeDtypeStruct((B, M, X_WIDTH), BF16),
        ],
        compiler_params=pltpu.CompilerParams(
            dimension_semantics=("arbitrary",), vmem_limit_bytes=VMEM_LIMIT),
        name="memkv",
    )(mem, ln_mem, w_mem_kv)


def _attn_kernel(qT_ref, k_ref, vT_ref, o_ref, acc_ref, m_ref, s_ref, p_ref, alpha_ref, cmax_ref, lead_ref, *,
                 tk, tv, n_chunks):
    def key_scores(c, hh):
        start = pl.multiple_of(c * tk, tk)
        ks = k_ref[0, 0, pl.ds(start, tk), :]
        return jnp.dot(ks, qT_ref[0, hh], preferred_element_type=F32)

    def pv_product(c, hh):
        pv = None
        for u in range(tk // tv):
            part = jnp.dot(vT_ref[0, 0, c * (tk // tv) + u], p_ref[hh % 2, u * tv:(u + 1) * tv, :],
                           preferred_element_type=F32)
            pv = part if pv is None else pv + part
        return pv

    def finalize():
        for hp in range(HEADS_PER_KV // 2):
            outs = []
            for hh in (2 * hp, 2 * hp + 1):
                a = acc_ref[hh]
                outs.append(a[:HEAD_DIM] / a[HEAD_DIM:HEAD_DIM + 1])
            o = jnp.concatenate(outs, axis=0).T
            o_ref[0, :, 128 * hp:128 * (hp + 1)] = o.astype(BF16)

    def spec_scores(c, hh, first):
        s = key_scores(c, hh)
        ref = s[0:1, :] if first else m_ref[hh]
        p_ref[hh % 2] = jnp.exp2((s - ref).astype(BF16))
        m_new = jnp.maximum(ref, jnp.max(s, axis=0, keepdims=True))
        alpha_ref[hh % 2] = jnp.exp2(ref - m_new)
        lead_ref[...] = jnp.maximum(lead_ref[...], m_new - ref)
        m_ref[hh] = m_new

    def spec_accumulate(c, hh):
        acc_ref[hh] = (acc_ref[hh] + pv_product(c, hh)) * alpha_ref[hh % 2]

    def spec_chunk(c, first, last):
        for hh in range(HEADS_PER_KV):
            if hh + 1 < HEADS_PER_KV:
                spec_scores(c, hh + 1, first)
            elif not last:
                spec_scores(c + 1, 0, False)
            spec_accumulate(c, hh)

    def spec_loop_body(c, carry):
        spec_chunk(c, first=False, last=False)
        return carry

    acc_ref[...] = jnp.zeros_like(acc_ref)
    lead_ref[...] = jnp.zeros_like(lead_ref)
    spec_scores(0, 0, True)
    spec_chunk(0, first=True, last=False)
    lax.fori_loop(1, n_chunks - 1, spec_loop_body, 0, unroll=CHUNK_UNROLL)
    spec_chunk(n_chunks - 1, first=False, last=True)
    finalize()

    def safe_scores(c, hh):
        s = key_scores(c, hh)
        s_ref[hh % 2] = s
        cmax_ref[hh % 2] = jnp.max(s, axis=0, keepdims=True)

    def safe_softmax(hh):
        m_old = m_ref[hh]
        m_new = jnp.maximum(m_old, cmax_ref[hh % 2])
        p_ref[hh % 2] = jnp.exp2(s_ref[hh % 2] - m_new).astype(BF16)
        alpha_ref[hh % 2] = jnp.exp2(m_old - m_new)
        m_ref[hh] = m_new

    def safe_accumulate(c, hh):
        acc_ref[hh] = acc_ref[hh] * alpha_ref[hh % 2] + pv_product(c, hh)

    def safe_chunk(c, last):
        for hh in range(HEADS_PER_KV):
            if hh + 2 < HEADS_PER_KV:
                safe_scores(c, hh + 2)
            elif not last:
                safe_scores(c + 1, hh + 2 - HEADS_PER_KV)
            if hh + 1 < HEADS_PER_KV:
                safe_softmax(hh + 1)
            elif not last:
                safe_softmax(0)
            safe_accumulate(c, hh)

    def safe_loop_body(c, carry):
        safe_chunk(c, last=False)
        return carry

    @pl.when(jnp.max(lead_ref[...]) > MAX_LEAD)
    def _exact_max_path():
        acc_ref[...] = jnp.zeros_like(acc_ref)
        m_ref[...] = jnp.full_like(m_ref, -jnp.inf)
        safe_scores(0, 0)
        safe_scores(0, 1)
        safe_softmax(0)
        lax.fori_loop(0, n_chunks - 1, safe_loop_body, 0)
        safe_chunk(n_chunks - 1, last=True)
        finalize()


def _attn_call(qT, kk, vT, tq, tk):
    B, _, _, L = qT.shape
    n_pieces, tv = vT.shape[2], vT.shape[4]
    n_chunks = L // tk
    kern = functools.partial(_attn_kernel, tk=tk, tv=tv, n_chunks=n_chunks)
    return pl.pallas_call(
        kern,
        grid=(B, N_KV_HEADS, L // tq),
        in_specs=[
            pl.BlockSpec((1, HEADS_PER_KV, HEAD_DIM, tq), lambda b, g, i: (b, g, 0, i)),
            pl.BlockSpec((1, 1, L, HEAD_DIM), lambda b, g, i: (b, g, 0, 0)),
            pl.BlockSpec((1, 1, n_pieces, V_ROWS, tv), lambda b, g, i: (b, g, 0, 0, 0)),
        ],
        out_specs=pl.BlockSpec((1, tq, HEADS_PER_KV * HEAD_DIM), lambda b, g, i: (b, i, g)),
        out_shape=jax.ShapeDtypeStruct((B, L, ATTN_WIDTH), BF16),
        scratch_shapes=[
            pltpu.VMEM((HEADS_PER_KV, V_ROWS, tq), F32),
            pltpu.VMEM((HEADS_PER_KV, 1, tq), F32),
            pltpu.VMEM((2, tk, tq), F32),
            pltpu.VMEM((2, tk, tq), BF16),
            pltpu.VMEM((2, 1, tq), F32),
            pltpu.VMEM((2, 1, tq), F32),
            pltpu.VMEM((1, tq), F32),
        ],
        compiler_params=pltpu.CompilerParams(
            dimension_semantics=("arbitrary", "arbitrary", "arbitrary"), vmem_limit_bytes=VMEM_LIMIT),
        name="attn",
    )(qT, kk, vT)


def _out_kernel(x_ref, zr_ref, att_ref, mkT_ref, mv_ref, wbr_ref, wout_ref, lnpost_ref, y_ref):
    br_pool = zr_ref[0, :, R_BR_POOL:R_BR_POOL + POOL_WIDTH]

    attn_silu = zr_ref[0, :, R_ATTN_GATE:R_ATTN_GATE + ATTN_WIDTH].astype(F32)
    br_attn = (att_ref[0].astype(F32) * attn_silu).astype(BF16)

    cross_parts = []
    for hh in range(N_X_HEADS):
        lo_c = hh * X_HEAD_DIM
        xq = zr_ref[0, :, R_XQ + lo_c:R_XQ + lo_c + X_HEAD_DIM]
        s = jnp.dot(xq, mkT_ref[0, lo_c:lo_c + X_HEAD_DIM, :], preferred_element_type=F32)
        p = jnp.exp2(s - jnp.max(s, axis=-1, keepdims=True))
        den = jnp.sum(p, axis=-1, keepdims=True)
        o = jnp.dot(p.astype(BF16), mv_ref[0, :, lo_c:lo_c + X_HEAD_DIM], preferred_element_type=F32)
        cross_parts.append(o / den)
    x_silu = zr_ref[0, :, R_X_GATE:R_X_GATE + X_WIDTH].astype(F32)
    br_cross = (jnp.concatenate(cross_parts, axis=1) * x_silu).astype(BF16)

    merged2 = None
    for n, br in enumerate((br_pool, br_attn, br_cross)):
        proj = jnp.dot(br, wbr_ref[n], preferred_element_type=F32)
        gate_tanh = zr_ref[0, :, R_MERGE + n * D_MODEL:R_MERGE + (n + 1) * D_MODEL].astype(F32)
        gated2 = proj + gate_tanh * proj
        merged2 = gated2 if merged2 is None else merged2 + gated2

    o = jnp.dot(merged2.astype(BF16), wout_ref[...], preferred_element_type=F32)
    ms = jnp.mean(o * o, axis=-1, keepdims=True)
    y_ref[0] = x_ref[0] + o * lax.rsqrt(ms + EPS) * lnpost_ref[...]


def _out_call(x, zr, att, mkT, mv, w_branch, w_out, ln_post, tm):
    B, L, _ = x.shape
    M = mv.shape[1]
    return pl.pallas_call(
        _out_kernel,
        grid=(B, L // tm),
        in_specs=[
            pl.BlockSpec((1, tm, D_MODEL), lambda b, i: (b, i, 0)),
            pl.BlockSpec((1, tm, REST_WIDTH), lambda b, i: (b, i, 0)),
            pl.BlockSpec((1, tm, ATTN_WIDTH), lambda b, i: (b, i, 0)),
            pl.BlockSpec((1, X_WIDTH, M), lambda b, i: (b, 0, 0)),
            pl.BlockSpec((1, M, X_WIDTH), lambda b, i: (b, 0, 0)),
            _const_spec((N_BRANCH, BRANCH_WIDTH, D_MODEL)),
            _const_spec((D_MODEL, D_MODEL)),
            _const_spec((1, D_MODEL)),
        ],
        out_specs=pl.BlockSpec((1, tm, D_MODEL), lambda b, i: (b, i, 0)),
        out_shape=jax.ShapeDtypeStruct((B, L, D_MODEL), F32),
        compiler_params=pltpu.CompilerParams(
            dimension_semantics=("arbitrary", "arbitrary"), vmem_limit_bytes=VMEM_LIMIT),
        name="mixout",
    )(x, zr, att, mkT, mv, w_branch, w_out, ln_post)


def _rope_tables(L):
    pos = jnp.arange(L, dtype=jnp.int32)
    row = (pos // GRID_W).astype(F32)
    col = (pos % GRID_W).astype(F32)
    axis_dim = HEAD_DIM // 2
    inv = ROPE_THETA ** (-jnp.arange(0, axis_dim, 2, dtype=F32) / axis_dim)
    ang = jnp.concatenate([row[:, None] * inv, col[:, None] * inv], axis=-1)
    cos = jnp.repeat(jnp.cos(ang), 2, axis=-1)
    sin = jnp.repeat(jnp.sin(ang), 2, axis=-1) * jnp.tile(jnp.array([-1.0, 1.0], F32), HEAD_DIM // 2)
    return jnp.tile(cos, (1, 2)), jnp.tile(sin, (1, 2))


def _layer(x, mem, prm, cos_t, sin_t, tm, tq, tk):
    qT, kk, vT, zr = _proj_call(x, prm["ln_pre"], prm["w_qkv"], prm["w_pg"], prm["w_rest"], prm["bd"], prm["qn"],
                                prm["kn"], cos_t, sin_t, prm["b_merge_row"], prm["w_pool"], prm["pool_scale"],
                                tm, min(tk, tm))
    mkT, mv = _mem_call(mem, prm["ln_mem"], prm["w_mem_kv"])
    att = _attn_call(qT, kk, vT, tq, tk)
    return _out_call(x, zr, att, mkT, mv, prm["w_branch"], prm["w_out"], prm["ln_post"], tm)


def _prep_params(ln_pre, ln_post, ln_mem, w_in, b_merge, q_norm, k_norm, w_pool, pool_scale, w_mem_kv,
                 w_branch, w_out):
    head_id = np.arange(ATTN_WIDTH) // HEAD_DIM
    bd = (head_id[:, None] == head_id[None, :]).astype(np.float32) / HEAD_DIM
    pg_scale = np.ones((PG_WIDTH,), np.float32)
    pg_scale[POOL_WIDTH:] = 0.5
    rest_scale = np.full((W_REST_WIDTH,), 0.5, np.float32)
    rest_scale[R_XQ - POOL_WIDTH:R_XQ - POOL_WIDTH + X_WIDTH] = LOG2_E * X_HEAD_DIM ** -0.5
    return {
        "ln_pre": ln_pre.reshape(1, D_MODEL),
        "ln_post": ln_post.reshape(1, D_MODEL),
        "ln_mem": ln_mem.reshape(1, D_MODEL),
        "w_qkv": w_in[:, _Q0:_AG0].astype(BF16),
        "w_pg": (w_in[:, :_Q0] * pg_scale).astype(BF16),
        "w_rest": (w_in[:, _AG0:] * rest_scale).astype(BF16),
        "bd": jnp.asarray(bd, BF16),
        "qn": jnp.tile(q_norm, N_HEADS).reshape(1, ATTN_WIDTH),
        "kn": jnp.tile(k_norm, N_KV_HEADS).reshape(1, KV_WIDTH),
        "w_pool": w_pool.astype(BF16),
        "pool_scale": pool_scale.reshape(1, POOL_WIDTH),
        "w_mem_kv": w_mem_kv.astype(BF16),
        "w_branch": w_branch.astype(BF16),
        "b_merge_row": b_merge.reshape(1, N_BRANCH * D_MODEL),
        "w_out": (0.5 * w_out).astype(BF16),
    }


def kernel(x_prompt, x_sample, mem_prompt, mem_sample, ln_pre, ln_post, ln_mem, w_in, b_merge, q_norm, k_norm,
           w_pool, pool_scale, w_mem_kv, w_branch, w_out):
    prm = _prep_params(ln_pre, ln_post, ln_mem, w_in, b_merge, q_norm, k_norm, w_pool, pool_scale, w_mem_kv,
                       w_branch, w_out)
    cos_t, sin_t = _rope_tables(max(x_prompt.shape[1], x_sample.shape[1]))
    outs = []
    for x, mem in ((x_prompt, mem_prompt), (x_sample, mem_sample)):
        L = x.shape[1]
        tm = min(512, L)
        tk = max(tm, L // ATTN_CHUNKS)
        outs.append(_layer(x, mem, prm, cos_t, sin_t, tm=tm, tq=min(512, L), tk=tk))
    return tuple(outs)
```

```python
import functools

import numpy as np
import jax
import jax.numpy as jnp
from jax import lax
from jax.experimental import pallas as pl
from jax.experimental.pallas import tpu as pltpu

F32 = jnp.float32
BF16 = jnp.bfloat16

D_MODEL = 1024
GRID_W = 64
EPS = 1e-6
POOL_GROUPS = 4
POOL_GROUP_DIM = 128
POOL_WIDTH = 512
POOL_WINDOWS = (2, 4, 8, 16)
N_HEADS = 8
N_KV_HEADS = 2
HEAD_DIM = 64
HEADS_PER_KV = N_HEADS // N_KV_HEADS
ATTN_WIDTH = 512
KV_WIDTH = 128
ROPE_THETA = 10000.0
LOG2_E = 1.4426950408889634
N_X_HEADS = 4
X_HEAD_DIM = 128
X_WIDTH = 512
N_BRANCH = 3
BRANCH_WIDTH = 512

_Q0 = 2 * POOL_WIDTH
_K0 = _Q0 + ATTN_WIDTH
_V0 = _K0 + KV_WIDTH
_AG0 = _V0 + KV_WIDTH
QKV_WIDTH = ATTN_WIDTH + 2 * KV_WIDTH
R_BR_POOL = 0
R_ATTN_GATE = 512
R_XQ = 1024
R_X_GATE = 1536
R_MERGE = 2048
REST_WIDTH = R_MERGE + N_BRANCH * D_MODEL
PG_WIDTH = 2 * POOL_WIDTH
W_REST_WIDTH = REST_WIDTH - POOL_WIDTH

V_ROWS = HEAD_DIM + 16
HALO = 16
VMEM_LIMIT = 56 * 1024 * 1024
CHUNK_UNROLL = 3
ATTN_CHUNKS = 4
MAX_LEAD = 64.0


def _const_spec(shape):
    n = len(shape)
    return pl.BlockSpec(shape, lambda *_: (0,) * n, pipeline_mode=pl.Buffered(1))


def _rope(t, c, s):
    lane = lax.broadcasted_iota(jnp.int32, t.shape, 1)
    nxt = pltpu.roll(t, t.shape[1] - 1, axis=1)
    prv = pltpu.roll(t, 1, axis=1)
    return t * c + jnp.where(lane % 2 == 0, nxt, prv) * s


def _shift_rows(a, k):
    return pltpu.roll(a, k % a.shape[0], axis=0)


def _proj_kernel(x_ref, xprev_ref, xnext_ref, lnpre_ref, wqkv_ref, wpg_ref, wrest_ref, bd_ref, kn_ref,
                 cos_ref, sin_ref, cosT_ref, sinT_ref, bmerge_ref, wpool_ref, pscale_ref,
                 qT_ref, k_ref, vT_ref, zr_ref, *, tm, seq_len):
    i = pl.program_id(1)
    nblk = pl.num_programs(1)

    def pre_norm(xv):
        ms = jnp.mean(xv * xv, axis=-1, keepdims=True)
        return (xv * lax.rsqrt(ms + EPS) * lnpre_ref[...]).astype(BF16)

    h = pre_norm(x_ref[0])

    qkv = jnp.dot(h, wqkv_ref[...], preferred_element_type=F32)
    q = qkv[:, :ATTN_WIDTH]
    k = qkv[:, ATTN_WIDTH:ATTN_WIDTH + KV_WIDTH]
    v = qkv[:, ATTN_WIDTH + KV_WIDTH:]

    cT = cosT_ref[...]
    sT = sinT_ref[...]
    row = lax.broadcasted_iota(jnp.int32, cT.shape, 0)
    for j in range(ATTN_WIDTH // 128):
        qt = q[:, 128 * j:128 * (j + 1)].T
        sq = qt * qt
        inv_rms = jnp.concatenate(
            [jnp.broadcast_to(lax.rsqrt(jnp.mean(sq[hf * HEAD_DIM:(hf + 1) * HEAD_DIM], axis=0, keepdims=True)
                                        + EPS), (HEAD_DIM, qt.shape[1])) for hf in range(2)], axis=0)
        nxt = pltpu.roll(qt, qt.shape[0] - 1, axis=0)
        prv = pltpu.roll(qt, 1, axis=0)
        rot = (qt * cT + jnp.where(row % 2 == 0, nxt, prv) * sT) * inv_rms
        qT_ref[0, 2 * j] = rot[:HEAD_DIM].astype(BF16)
        qT_ref[0, 2 * j + 1] = rot[HEAD_DIM:].astype(BF16)

    k_ms = jnp.dot((k * k).astype(BF16), bd_ref[...], preferred_element_type=F32)
    kn = k * lax.rsqrt(k_ms + EPS) * kn_ref[...]
    kr = _rope(kn, cos_ref[...], sin_ref[...]).astype(BF16)
    k_ref[0, 0] = kr[:, :HEAD_DIM]
    k_ref[0, 1] = kr[:, HEAD_DIM:]
    vt = v.T.astype(BF16)
    tk = vT_ref.shape[4]
    ones = jnp.ones((V_ROWS - HEAD_DIM, tk), BF16)
    for g in range(N_KV_HEADS):
        for cc in range(vT_ref.shape[2]):
            vT_ref[0, g, cc, :HEAD_DIM] = vt[g * HEAD_DIM:(g + 1) * HEAD_DIM, cc * tk:(cc + 1) * tk]
            vT_ref[0, g, cc, HEAD_DIM:] = ones

    h_ext = jnp.concatenate([pre_norm(xprev_ref[0]), h, pre_norm(xnext_ref[0])], axis=0)
    u_ext = jnp.dot(h_ext, wpg_ref[:, :POOL_WIDTH], preferred_element_type=F32)
    prev_ok = (i > 0).astype(F32)
    next_ok = (i < nblk - 1).astype(F32)
    u_ext = jnp.concatenate([u_ext[:HALO] * prev_ok, u_ext[HALO:HALO + tm], u_ext[HALO + tm:] * next_ok], axis=0)
    t = i * tm + lax.broadcasted_iota(jnp.int32, (tm, POOL_GROUP_DIM), 0)

    def pooled_group(g):
        w = POOL_WINDOWS[g]
        a = u_ext[:, g * POOL_GROUP_DIM:(g + 1) * POOL_GROUP_DIM]
        ssum = a + _shift_rows(a, 1)
        reach = 1
        while 2 * reach < w:
            ssum = _shift_rows(ssum, reach) + _shift_rows(ssum, -reach)
            reach *= 2
        ssum = ssum[HALO:HALO + tm]
        lo = jnp.maximum(t - w // 2, 0)
        hi = jnp.minimum(t + (w - w // 2), seq_len)
        cnt = (hi - lo).astype(F32)
        mixed = (ssum / cnt - a[HALO:HALO + tm]).astype(BF16)
        return jnp.dot(mixed, wpool_ref[g], preferred_element_type=F32)

    pooled_parts = []
    for idx, c0 in enumerate(range(0, W_REST_WIDTH, 512)):
        z = jnp.dot(h, wrest_ref[:, c0:c0 + 512], preferred_element_type=F32)
        if idx % 2 == 0 and idx // 2 < POOL_GROUPS:
            pooled_parts.append(pooled_group(idx // 2))
        col = POOL_WIDTH + c0
        if col in (R_ATTN_GATE, R_X_GATE):
            z = z + z * jnp.tanh(z)
        elif col >= R_MERGE:
            z = jnp.tanh(z + 0.5 * bmerge_ref[:, col - R_MERGE:col - R_MERGE + 512])
        zr_ref[0, :, col:col + 512] = z.astype(BF16)

    pool_gate_h = jnp.dot(h, wpg_ref[:, POOL_WIDTH:], preferred_element_type=F32)
    pool_silu = pool_gate_h + pool_gate_h * jnp.tanh(pool_gate_h)
    br_pool = jnp.concatenate(pooled_parts, axis=1) * pscale_ref[...] * pool_silu
    zr_ref[0, :, R_BR_POOL:R_BR_POOL + POOL_WIDTH] = br_pool.astype(BF16)


def _proj_call(x, ln_pre, w_qkv, w_pg, w_rest, bd, kn, cos_t, sin_t, cos_qT, sin_qT, b_merge_row, w_pool,
               pool_scale, tm, tk):
    B, L, _ = x.shape
    nblk = L // tm
    cpb = tm // tk
    hb = tm // HALO
    last_halo = L // HALO - 1
    kern = functools.partial(_proj_kernel, tm=tm, seq_len=L)
    return pl.pallas_call(
        kern,
        grid=(B, nblk),
        in_specs=[
            pl.BlockSpec((1, tm, D_MODEL), lambda b, i: (b, i, 0)),
            pl.BlockSpec((1, HALO, D_MODEL), lambda b, i: (b, jnp.maximum(i * hb - 1, 0), 0)),
            pl.BlockSpec((1, HALO, D_MODEL), lambda b, i: (b, jnp.minimum((i + 1) * hb, last_halo), 0)),
            _const_spec((1, D_MODEL)),
            _const_spec((D_MODEL, QKV_WIDTH)),
            _const_spec((D_MODEL, PG_WIDTH)),
            _const_spec((D_MODEL, W_REST_WIDTH)),
            _const_spec((KV_WIDTH, KV_WIDTH)),
            _const_spec((1, KV_WIDTH)),
            pl.BlockSpec((tm, 128), lambda b, i: (i, 0)),
            pl.BlockSpec((tm, 128), lambda b, i: (i, 0)),
            pl.BlockSpec((128, tm), lambda b, i: (0, i)),
            pl.BlockSpec((128, tm), lambda b, i: (0, i)),
            _const_spec((1, N_BRANCH * D_MODEL)),
            _const_spec((POOL_GROUPS, POOL_GROUP_DIM, POOL_GROUP_DIM)),
            _const_spec((1, POOL_WIDTH)),
        ],
        out_specs=[
            pl.BlockSpec((1, N_HEADS, HEAD_DIM, tm), lambda b, i: (b, 0, 0, i)),
            pl.BlockSpec((1, N_KV_HEADS, tm, HEAD_DIM), lambda b, i: (b, 0, i, 0)),
            pl.BlockSpec((1, N_KV_HEADS, cpb, V_ROWS, tk), lambda b, i: (b, 0, i, 0, 0)),
            pl.BlockSpec((1, tm, REST_WIDTH), lambda b, i: (b, i, 0)),
        ],
        out_shape=[
            jax.ShapeDtypeStruct((B, N_HEADS, HEAD_DIM, L), BF16),
            jax.ShapeDtypeStruct((B, N_KV_HEADS, L, HEAD_DIM), BF16),
            jax.ShapeDtypeStruct((B, N_KV_HEADS, L // tk, V_ROWS, tk), BF16),
            jax.ShapeDtypeStruct((B, L, REST_WIDTH), BF16),
        ],
        compiler_params=pltpu.CompilerParams(
            dimension_semantics=("arbitrary", "arbitrary"), vmem_limit_bytes=VMEM_LIMIT),
        name="proj",
    )(x, x, x, ln_pre, w_qkv, w_pg, w_rest, bd, kn, cos_t, sin_t, cos_qT, sin_qT, b_merge_row, w_pool, pool_scale)


def _mem_kernel(mem_ref, lnmem_ref, wkv_ref, mkT_ref, mv_ref):
    m = mem_ref[0]
    ms = jnp.mean(m * m, axis=-1, keepdims=True)
    mn = (m * lax.rsqrt(ms + EPS) * lnmem_ref[...]).astype(BF16)
    kv = jnp.dot(mn, wkv_ref[...], preferred_element_type=F32)
    mkT_ref[0] = kv[:, :X_WIDTH].T.astype(BF16)
    mv_ref[0] = kv[:, X_WIDTH:].astype(BF16)


def _mem_call(mem, ln_mem, w_mem_kv):
    B, M, _ = mem.shape
    return pl.pallas_call(
        _mem_kernel,
        grid=(B,),
        in_specs=[
            pl.BlockSpec((1, M, D_MODEL), lambda b: (b, 0, 0)),
            _const_spec((1, D_MODEL)),
            _const_spec((D_MODEL, 2 * X_WIDTH)),
        ],
        out_specs=[
            pl.BlockSpec((1, X_WIDTH, M), lambda b: (b, 0, 0)),
            pl.BlockSpec((1, M, X_WIDTH), lambda b: (b, 0, 0)),
        ],
        out_shape=[
            jax.ShapeDtypeStruct((B, X_WIDTH, M), BF16),
            jax.ShapeDtypeStruct((B, M, X_WIDTH), BF16),
        ],
        compiler_params=pltpu.CompilerParams(
            dimension_semantics=("arbitrary",), vmem_limit_bytes=VMEM_LIMIT),
        name="memkv",
    )(mem, ln_mem, w_mem_kv)


def _attn_kernel(qT_ref, k_ref, vT_ref, o_ref, acc_ref, m_ref, s_ref, p_ref, alpha_ref, cmax_ref, lead_ref, *,
                 tk, tv, n_chunks):
    def key_scores(c, hh):
        start = pl.multiple_of(c * tk, tk)
        ks = k_ref[0, 0, pl.ds(start, tk), :]
        return jnp.dot(ks, qT_ref[0, hh], preferred_element_type=F32)

    def pv_product(c, hh):
        pv = None
        for u in range(tk // tv):
            part = jnp.dot(vT_ref[0, 0, c * (tk // tv) + u], p_ref[hh % 2, u * tv:(u + 1) * tv, :],
                           preferred_element_type=F32)
            pv = part if pv is None else pv + part
        return pv

    def finalize():
        for hp in range(HEADS_PER_KV // 2):
            outs = []
            for hh in (2 * hp, 2 * hp + 1):
                a = acc_ref[hh]
                outs.append(a[:HEAD_DIM] / a[HEAD_DIM:HEAD_DIM + 1])
            o = jnp.concatenate(outs, axis=0).T
            o_ref[0, :, 128 * hp:128 * (hp + 1)] = o.astype(BF16)

    def spec_scores(c, hh, first):
        s = key_scores(c, hh)
        ref = s[0:1, :] if first else m_ref[hh]
        p_ref[hh % 2] = jnp.exp2((s - ref).astype(BF16))
        m_new = jnp.maximum(ref, jnp.max(s, axis=0, keepdims=True))
        alpha_ref[hh % 2] = jnp.exp2(ref - m_new)
        lead_ref[...] = jnp.maximum(lead_ref[...], m_new - ref)
        m_ref[hh] = m_new

    def spec_accumulate(c, hh):
        acc_ref[hh] = (acc_ref[hh] + pv_product(c, hh)) * alpha_ref[hh % 2]

    def spec_chunk(c, first, last):
        for hh in range(HEADS_PER_KV):
            if hh + 1 < HEADS_PER_KV:
                spec_scores(c, hh + 1, first)
            elif not last:
                spec_scores(c + 1, 0, False)
            spec_accumulate(c, hh)

    def spec_loop_body(c, carry):
        spec_chunk(c, first=False, last=False)
        return carry

    acc_ref[...] = jnp.zeros_like(acc_ref)
    lead_ref[...] = jnp.zeros_like(lead_ref)
    spec_scores(0, 0, True)
    spec_chunk(0, first=True, last=False)
    lax.fori_loop(1, n_chunks - 1, spec_loop_body, 0, unroll=CHUNK_UNROLL)
    spec_chunk(n_chunks - 1, first=False, last=True)
    finalize()

    def safe_scores(c, hh):
        s = key_scores(c, hh)
        s_ref[hh % 2] = s
        cmax_ref[hh % 2] = jnp.max(s, axis=0, keepdims=True)

    def safe_softmax(hh):
        m_old = m_ref[hh]
        m_new = jnp.maximum(m_old, cmax_ref[hh % 2])
        p_ref[hh % 2] = jnp.exp2(s_ref[hh % 2] - m_new).astype(BF16)
        alpha_ref[hh % 2] = jnp.exp2(m_old - m_new)
        m_ref[hh] = m_new

    def safe_accumulate(c, hh):
        acc_ref[hh] = acc_ref[hh] * alpha_ref[hh % 2] + pv_product(c, hh)

    def safe_chunk(c, last):
        for hh in range(HEADS_PER_KV):
            if hh + 2 < HEADS_PER_KV:
                safe_scores(c, hh + 2)
            elif not last:
                safe_scores(c + 1, hh + 2 - HEADS_PER_KV)
            if hh + 1 < HEADS_PER_KV:
                safe_softmax(hh + 1)
            elif not last:
                safe_softmax(0)
            safe_accumulate(c, hh)

    def safe_loop_body(c, carry):
        safe_chunk(c, last=False)
        return carry

    @pl.when(jnp.max(lead_ref[...]) > MAX_LEAD)
    def _exact_max_path():
        acc_ref[...] = jnp.zeros_like(acc_ref)
        m_ref[...] = jnp.full_like(m_ref, -jnp.inf)
        safe_scores(0, 0)
        safe_scores(0, 1)
        safe_softmax(0)
        lax.fori_loop(0, n_chunks - 1, safe_loop_body, 0)
        safe_chunk(n_chunks - 1, last=True)
        finalize()


def _attn_call(qT, kk, vT, tq, tk):
    B, _, _, L = qT.shape
    n_pieces, tv = vT.shape[2], vT.shape[4]
    n_chunks = L // tk
    kern = functools.partial(_attn_kernel, tk=tk, tv=tv, n_chunks=n_chunks)
    return pl.pallas_call(
        kern,
        grid=(B, N_KV_HEADS, L // tq),
        in_specs=[
            pl.BlockSpec((1, HEADS_PER_KV, HEAD_DIM, tq), lambda b, g, i: (b, g, 0, i)),
            pl.BlockSpec((1, 1, L, HEAD_DIM), lambda b, g, i: (b, g, 0, 0)),
            pl.BlockSpec((1, 1, n_pieces, V_ROWS, tv), lambda b, g, i: (b, g, 0, 0, 0)),
        ],
        out_specs=pl.BlockSpec((1, tq, HEADS_PER_KV * HEAD_DIM), lambda b, g, i: (b, i, g)),
        out_shape=jax.ShapeDtypeStruct((B, L, ATTN_WIDTH), BF16),
        scratch_shapes=[
            pltpu.VMEM((HEADS_PER_KV, V_ROWS, tq), F32),
            pltpu.VMEM((HEADS_PER_KV, 1, tq), F32),
            pltpu.VMEM((2, tk, tq), F32),
            pltpu.VMEM((2, tk, tq), BF16),
            pltpu.VMEM((2, 1, tq), F32),
            pltpu.VMEM((2, 1, tq), F32),
            pltpu.VMEM((1, tq), F32),
        ],
        compiler_params=pltpu.CompilerParams(
            dimension_semantics=("arbitrary", "arbitrary", "arbitrary"), vmem_limit_bytes=VMEM_LIMIT),
        name="attn",
    )(qT, kk, vT)


def _out_kernel(x_ref, zr_ref, att_ref, mkT_ref, mv_ref, wbr_ref, wout_ref, lnpost_ref, y_ref):
    br_pool = zr_ref[0, :, R_BR_POOL:R_BR_POOL + POOL_WIDTH]

    attn_silu = zr_ref[0, :, R_ATTN_GATE:R_ATTN_GATE + ATTN_WIDTH].astype(F32)
    br_attn = (att_ref[0].astype(F32) * attn_silu).astype(BF16)

    cross_parts = []
    for hh in range(N_X_HEADS):
        lo_c = hh * X_HEAD_DIM
        xq = zr_ref[0, :, R_XQ + lo_c:R_XQ + lo_c + X_HEAD_DIM]
        s = jnp.dot(xq, mkT_ref[0, lo_c:lo_c + X_HEAD_DIM, :], preferred_element_type=F32)
        p = jnp.exp2(s - jnp.max(s, axis=-1, keepdims=True))
        den = jnp.sum(p, axis=-1, keepdims=True)
        o = jnp.dot(p.astype(BF16), mv_ref[0, :, lo_c:lo_c + X_HEAD_DIM], preferred_element_type=F32)
        cross_parts.append(o / den)
    x_silu = zr_ref[0, :, R_X_GATE:R_X_GATE + X_WIDTH].astype(F32)
    br_cross = (jnp.concatenate(cross_parts, axis=1) * x_silu).astype(BF16)

    merged2 = None
    for n, br in enumerate((br_pool, br_attn, br_cross)):
        proj = jnp.dot(br, wbr_ref[n], preferred_element_type=F32)
        gate_tanh = zr_ref[0, :, R_MERGE + n * D_MODEL:R_MERGE + (n + 1) * D_MODEL].astype(F32)
        gated2 = proj + gate_tanh * proj
        merged2 = gated2 if merged2 is None else merged2 + gated2

    o = jnp.dot(merged2.astype(BF16), wout_ref[...], preferred_element_type=F32)
    ms = jnp.mean(o * o, axis=-1, keepdims=True)
    y_ref[0] = x_ref[0] + o * lax.rsqrt(ms + EPS) * lnpost_ref[...]


def _out_call(x, zr, att, mkT, mv, w_branch, w_out, ln_post, tm):
    B, L, _ = x.shape
    M = mv.shape[1]
    return pl.pallas_call(
        _out_kernel,
        grid=(B, L // tm),
        in_specs=[
            pl.BlockSpec((1, tm, D_MODEL), lambda b, i: (b, i, 0)),
            pl.BlockSpec((1, tm, REST_WIDTH), lambda b, i: (b, i, 0)),
            pl.BlockSpec((1, tm, ATTN_WIDTH), lambda b, i: (b, i, 0)),
            pl.BlockSpec((1, X_WIDTH, M), lambda b, i: (b, 0, 0)),
            pl.BlockSpec((1, M, X_WIDTH), lambda b, i: (b, 0, 0)),
            _const_spec((N_BRANCH, BRANCH_WIDTH, D_MODEL)),
            _const_spec((D_MODEL, D_MODEL)),
            _const_spec((1, D_MODEL)),
        ],
        out_specs=pl.BlockSpec((1, tm, D_MODEL), lambda b, i: (b, i, 0)),
        out_shape=jax.ShapeDtypeStruct((B, L, D_MODEL), F32),
        compiler_params=pltpu.CompilerParams(
            dimension_semantics=("arbitrary", "arbitrary"), vmem_limit_bytes=VMEM_LIMIT),
        name="mixout",
    )(x, zr, att, mkT, mv, w_branch, w_out, ln_post)


def _rope_tables(L, q_norm):
    pos = jnp.arange(L, dtype=jnp.int32)
    row = (pos // GRID_W).astype(F32)
    col = (pos % GRID_W).astype(F32)
    axis_dim = HEAD_DIM // 2
    inv = ROPE_THETA ** (-jnp.arange(0, axis_dim, 2, dtype=F32) / axis_dim)
    ang = jnp.concatenate([row[:, None] * inv, col[:, None] * inv], axis=-1)
    cos = jnp.repeat(jnp.cos(ang), 2, axis=-1)
    sin = jnp.repeat(jnp.sin(ang), 2, axis=-1) * jnp.tile(jnp.array([-1.0, 1.0], F32), HEAD_DIM // 2)
    g = q_norm.astype(F32) * (LOG2_E * HEAD_DIM ** -0.5)
    g_swapped = g.reshape(HEAD_DIM // 2, 2)[:, ::-1].reshape(HEAD_DIM)
    cos_qT = jnp.tile((cos * g).T, (2, 1))
    sin_qT = jnp.tile((sin * g_swapped).T, (2, 1))
    return jnp.tile(cos, (1, 2)), jnp.tile(sin, (1, 2)), cos_qT, sin_qT


def _layer(x, mem, prm, tables, tm, tq, tk):
    cos_t, sin_t, cos_qT, sin_qT = tables
    qT, kk, vT, zr = _proj_call(x, prm["ln_pre"], prm["w_qkv"], prm["w_pg"], prm["w_rest"], prm["bd"],
                                prm["kn"], cos_t, sin_t, cos_qT, sin_qT, prm["b_merge_row"], prm["w_pool"],
                                prm["pool_scale"], tm, min(tk, tm))
    mkT, mv = _mem_call(mem, prm["ln_mem"], prm["w_mem_kv"])
    att = _attn_call(qT, kk, vT, tq, tk)
    return _out_call(x, zr, att, mkT, mv, prm["w_branch"], prm["w_out"], prm["ln_post"], tm)


def _prep_params(ln_pre, ln_post, ln_mem, w_in, b_merge, q_norm, k_norm, w_pool, pool_scale, w_mem_kv,
                 w_branch, w_out):
    head_id = np.arange(KV_WIDTH) // HEAD_DIM
    bd = (head_id[:, None] == head_id[None, :]).astype(np.float32) / HEAD_DIM
    pg_scale = np.ones((PG_WIDTH,), np.float32)
    pg_scale[POOL_WIDTH:] = 0.5
    rest_scale = np.full((W_REST_WIDTH,), 0.5, np.float32)
    rest_scale[R_XQ - POOL_WIDTH:R_XQ - POOL_WIDTH + X_WIDTH] = LOG2_E * X_HEAD_DIM ** -0.5
    return {
        "ln_pre": ln_pre.reshape(1, D_MODEL),
        "ln_post": ln_post.reshape(1, D_MODEL),
        "ln_mem": ln_mem.reshape(1, D_MODEL),
        "w_qkv": w_in[:, _Q0:_AG0].astype(BF16),
        "w_pg": (w_in[:, :_Q0] * pg_scale).astype(BF16),
        "w_rest": (w_in[:, _AG0:] * rest_scale).astype(BF16),
        "bd": jnp.asarray(bd, BF16),
        "kn": jnp.tile(k_norm, N_KV_HEADS).reshape(1, KV_WIDTH),
        "w_pool": w_pool.astype(BF16),
        "pool_scale": pool_scale.reshape(1, POOL_WIDTH),
        "w_mem_kv": w_mem_kv.astype(BF16),
        "w_branch": w_branch.astype(BF16),
        "b_merge_row": b_merge.reshape(1, N_BRANCH * D_MODEL),
        "w_out": (0.5 * w_out).astype(BF16),
    }


def kernel(x_prompt, x_sample, mem_prompt, mem_sample, ln_pre, ln_post, ln_mem, w_in, b_merge, q_norm, k_norm,
           w_pool, pool_scale, w_mem_kv, w_branch, w_out):
    prm = _prep_params(ln_pre, ln_post, ln_mem, w_in, b_merge, q_norm, k_norm, w_pool, pool_scale, w_mem_kv,
                       w_branch, w_out)
    tables = _rope_tables(max(x_prompt.shape[1], x_sample.shape[1]), q_norm)
    outs = []
    for x, mem in ((x_prompt, mem_prompt), (x_sample, mem_sample)):
        L = x.shape[1]
        tm = min(512, L)
        tk = max(tm, L // ATTN_CHUNKS)
        outs.append(_layer(x, mem, prm, tables, tm=tm, tq=min(512, L), tk=tk))
    return tuple(outs)
```

```python
import functools

import numpy as np
import jax
import jax.numpy as jnp
from jax import lax
from jax.experimental import pallas as pl
from jax.experimental.pallas import tpu as pltpu

F32 = jnp.float32
BF16 = jnp.bfloat16

D_MODEL = 1024
GRID_W = 64
EPS = 1e-6
POOL_GROUPS = 4
POOL_GROUP_DIM = 128
POOL_WIDTH = 512
POOL_WINDOWS = (2, 4, 8, 16)
N_HEADS = 8
N_KV_HEADS = 2
HEAD_DIM = 64
HEADS_PER_KV = N_HEADS // N_KV_HEADS
ATTN_WIDTH = 512
KV_WIDTH = 128
ROPE_THETA = 10000.0
LOG2_E = 1.4426950408889634
N_X_HEADS = 4
X_HEAD_DIM = 128
X_WIDTH = 512
N_BRANCH = 3
BRANCH_WIDTH = 512

_Q0 = 2 * POOL_WIDTH
_K0 = _Q0 + ATTN_WIDTH
_V0 = _K0 + KV_WIDTH
_AG0 = _V0 + KV_WIDTH
QKV_WIDTH = ATTN_WIDTH + 2 * KV_WIDTH
R_BR_POOL = 0
R_ATTN_GATE = 512
R_XQ = 1024
R_X_GATE = 1536
R_MERGE = 2048
REST_WIDTH = R_MERGE + N_BRANCH * D_MODEL
PG_WIDTH = 2 * POOL_WIDTH
W_REST_WIDTH = REST_WIDTH - POOL_WIDTH

V_ROWS = HEAD_DIM + 16
HALO = 16
VMEM_LIMIT = 56 * 1024 * 1024
CHUNK_UNROLL = 3
ATTN_CHUNKS = 4
MAX_LEAD = 64.0


def _const_spec(shape):
    n = len(shape)
    return pl.BlockSpec(shape, lambda *_: (0,) * n, pipeline_mode=pl.Buffered(1))


def _rope(t, c, s):
    lane = lax.broadcasted_iota(jnp.int32, t.shape, 1)
    nxt = pltpu.roll(t, t.shape[1] - 1, axis=1)
    prv = pltpu.roll(t, 1, axis=1)
    return t * c + jnp.where(lane % 2 == 0, nxt, prv) * s


def _shift_rows(a, k):
    return pltpu.roll(a, k % a.shape[0], axis=0)


def _proj_kernel(x_ref, xprev_ref, xnext_ref, lnpre_ref, wqkv_ref, wpg_ref, wrest_ref, bd_ref, kn_ref,
                 cos_ref, sin_ref, cosT_ref, sinT_ref, bmerge_ref, wpool_ref, pscale_ref,
                 qT_ref, k_ref, vT_ref, zr_ref, *, tm, seq_len):
    i = pl.program_id(1)
    nblk = pl.num_programs(1)

    def pre_norm(xv):
        ms = jnp.mean(xv * xv, axis=-1, keepdims=True)
        return (xv * lax.rsqrt(ms + EPS) * lnpre_ref[...]).astype(BF16)

    h = pre_norm(x_ref[0])

    qkv = jnp.dot(h, wqkv_ref[...], preferred_element_type=F32)
    q = qkv[:, :ATTN_WIDTH]
    k = qkv[:, ATTN_WIDTH:ATTN_WIDTH + KV_WIDTH]
    v = qkv[:, ATTN_WIDTH + KV_WIDTH:]

    cT = cosT_ref[...]
    sT = sinT_ref[...]
    row = lax.broadcasted_iota(jnp.int32, cT.shape, 0)
    for j in range(ATTN_WIDTH // 128):
        qt = q[:, 128 * j:128 * (j + 1)].T
        sq = qt * qt
        inv_rms = jnp.concatenate(
            [jnp.broadcast_to(lax.rsqrt(jnp.mean(sq[hf * HEAD_DIM:(hf + 1) * HEAD_DIM], axis=0, keepdims=True)
                                        + EPS), (HEAD_DIM, qt.shape[1])) for hf in range(2)], axis=0)
        nxt = pltpu.roll(qt, qt.shape[0] - 1, axis=0)
        prv = pltpu.roll(qt, 1, axis=0)
        rot = (qt * cT + jnp.where(row % 2 == 0, nxt, prv) * sT) * inv_rms
        qT_ref[0, 2 * j] = rot[:HEAD_DIM].astype(BF16)
        qT_ref[0, 2 * j + 1] = rot[HEAD_DIM:].astype(BF16)

    k_ms = jnp.dot((k * k).astype(BF16), bd_ref[...], preferred_element_type=F32)
    kn = k * lax.rsqrt(k_ms + EPS) * kn_ref[...]
    kr = _rope(kn, cos_ref[...], sin_ref[...]).astype(BF16)
    k_ref[0, 0] = kr[:, :HEAD_DIM]
    k_ref[0, 1] = kr[:, HEAD_DIM:]
    vt = v.T.astype(BF16)
    tk = vT_ref.shape[4]
    ones = jnp.ones((V_ROWS - HEAD_DIM, tk), BF16)
    for g in range(N_KV_HEADS):
        for cc in range(vT_ref.shape[2]):
            vT_ref[0, g, cc, :HEAD_DIM] = vt[g * HEAD_DIM:(g + 1) * HEAD_DIM, cc * tk:(cc + 1) * tk]
            vT_ref[0, g, cc, HEAD_DIM:] = ones

    h_ext = jnp.concatenate([pre_norm(xprev_ref[0]), h, pre_norm(xnext_ref[0])], axis=0)
    u_ext = jnp.dot(h_ext, wpg_ref[:, :POOL_WIDTH], preferred_element_type=F32)
    prev_ok = (i > 0).astype(F32)
    next_ok = (i < nblk - 1).astype(F32)
    u_ext = jnp.concatenate([u_ext[:HALO] * prev_ok, u_ext[HALO:HALO + tm], u_ext[HALO + tm:] * next_ok], axis=0)
    t = i * tm + lax.broadcasted_iota(jnp.int32, (tm, POOL_GROUP_DIM), 0)

    def pooled_group(g):
        w = POOL_WINDOWS[g]
        a = u_ext[:, g * POOL_GROUP_DIM:(g + 1) * POOL_GROUP_DIM]
        ssum = a + _shift_rows(a, 1)
        reach = 1
        while 2 * reach < w:
            ssum = _shift_rows(ssum, reach) + _shift_rows(ssum, -reach)
            reach *= 2
        ssum = ssum[HALO:HALO + tm]
        lo = jnp.maximum(t - w // 2, 0)
        hi = jnp.minimum(t + (w - w // 2), seq_len)
        cnt = (hi - lo).astype(F32)
        mixed = (ssum / cnt - a[HALO:HALO + tm]).astype(BF16)
        return jnp.dot(mixed, wpool_ref[g], preferred_element_type=F32)

    pooled_parts = []
    for idx, c0 in enumerate(range(0, W_REST_WIDTH, 512)):
        z = jnp.dot(h, wrest_ref[:, c0:c0 + 512], preferred_element_type=F32)
        if idx % 2 == 0 and idx // 2 < POOL_GROUPS:
            pooled_parts.append(pooled_group(idx // 2))
        col = POOL_WIDTH + c0
        if col in (R_ATTN_GATE, R_X_GATE):
            z = z + z * jnp.tanh(z)
        elif col >= R_MERGE:
            z = jnp.tanh(z + 0.5 * bmerge_ref[:, col - R_MERGE:col - R_MERGE + 512])
        zr_ref[0, :, col:col + 512] = z.astype(BF16)

    pool_gate_h = jnp.dot(h, wpg_ref[:, POOL_WIDTH:], preferred_element_type=F32)
    pool_silu = pool_gate_h + pool_gate_h * jnp.tanh(pool_gate_h)
    br_pool = jnp.concatenate(pooled_parts, axis=1) * pscale_ref[...] * pool_silu
    zr_ref[0, :, R_BR_POOL:R_BR_POOL + POOL_WIDTH] = br_pool.astype(BF16)


def _proj_call(x, ln_pre, w_qkv, w_pg, w_rest, bd, kn, cos_t, sin_t, cos_qT, sin_qT, b_merge_row, w_pool,
               pool_scale, tm, tk):
    B, L, _ = x.shape
    nblk = L // tm
    cpb = tm // tk
    hb = tm // HALO
    last_halo = L // HALO - 1
    kern = functools.partial(_proj_kernel, tm=tm, seq_len=L)
    return pl.pallas_call(
        kern,
        grid=(B, nblk),
        in_specs=[
            pl.BlockSpec((1, tm, D_MODEL), lambda b, i: (b, i, 0)),
            pl.BlockSpec((1, HALO, D_MODEL), lambda b, i: (b, jnp.maximum(i * hb - 1, 0), 0)),
            pl.BlockSpec((1, HALO, D_MODEL), lambda b, i: (b, jnp.minimum((i + 1) * hb, last_halo), 0)),
            _const_spec((1, D_MODEL)),
            _const_spec((D_MODEL, QKV_WIDTH)),
            _const_spec((D_MODEL, PG_WIDTH)),
            _const_spec((D_MODEL, W_REST_WIDTH)),
            _const_spec((KV_WIDTH, KV_WIDTH)),
            _const_spec((1, KV_WIDTH)),
            pl.BlockSpec((tm, 128), lambda b, i: (i, 0)),
            pl.BlockSpec((tm, 128), lambda b, i: (i, 0)),
            pl.BlockSpec((128, tm), lambda b, i: (0, i)),
            pl.BlockSpec((128, tm), lambda b, i: (0, i)),
            _const_spec((1, N_BRANCH * D_MODEL)),
            _const_spec((POOL_GROUPS, POOL_GROUP_DIM, POOL_GROUP_DIM)),
            _const_spec((1, POOL_WIDTH)),
        ],
        out_specs=[
            pl.BlockSpec((1, N_HEADS, HEAD_DIM, tm), lambda b, i: (b, 0, 0, i)),
            pl.BlockSpec((1, N_KV_HEADS, tm, HEAD_DIM), lambda b, i: (b, 0, i, 0)),
            pl.BlockSpec((1, N_KV_HEADS, cpb, V_ROWS, tk), lambda b, i: (b, 0, i, 0, 0)),
            pl.BlockSpec((1, tm, REST_WIDTH), lambda b, i: (b, i, 0)),
        ],
        out_shape=[
            jax.ShapeDtypeStruct((B, N_HEADS, HEAD_DIM, L), BF16),
            jax.ShapeDtypeStruct((B, N_KV_HEADS, L, HEAD_DIM), BF16),
            jax.ShapeDtypeStruct((B, N_KV_HEADS, L // tk, V_ROWS, tk), BF16),
            jax.ShapeDtypeStruct((B, L, REST_WIDTH), BF16),
        ],
        compiler_params=pltpu.CompilerParams(
            dimension_semantics=("arbitrary", "arbitrary"), vmem_limit_bytes=VMEM_LIMIT),
        name="proj",
    )(x, x, x, ln_pre, w_qkv, w_pg, w_rest, bd, kn, cos_t, sin_t, cos_qT, sin_qT, b_merge_row, w_pool, pool_scale)


def _mem_kernel(mem_ref, lnmem_ref, wkv_ref, mkT_ref, mv_ref):
    m = mem_ref[0]
    ms = jnp.mean(m * m, axis=-1, keepdims=True)
    mn = (m * lax.rsqrt(ms + EPS) * lnmem_ref[...]).astype(BF16)
    kv = jnp.dot(mn, wkv_ref[...], preferred_element_type=F32)
    mkT_ref[0] = kv[:, :X_WIDTH].T.astype(BF16)
    mv_ref[0] = kv[:, X_WIDTH:].astype(BF16)


def _mem_call(mem, ln_mem, w_mem_kv):
    B, M, _ = mem.shape
    return pl.pallas_call(
        _mem_kernel,
        grid=(B,),
        in_specs=[
            pl.BlockSpec((1, M, D_MODEL), lambda b: (b, 0, 0)),
            _const_spec((1, D_MODEL)),
            _const_spec((D_MODEL, 2 * X_WIDTH)),
        ],
        out_specs=[
            pl.BlockSpec((1, X_WIDTH, M), lambda b: (b, 0, 0)),
            pl.BlockSpec((1, M, X_WIDTH), lambda b: (b, 0, 0)),
        ],
        out_shape=[
            jax.ShapeDtypeStruct((B, X_WIDTH, M), BF16),
            jax.ShapeDtypeStruct((B, M, X_WIDTH), BF16),
        ],
        compiler_params=pltpu.CompilerParams(
            dimension_semantics=("arbitrary",), vmem_limit_bytes=VMEM_LIMIT),
        name="memkv",
    )(mem, ln_mem, w_mem_kv)


def _attn_kernel(qT_ref, k_ref, vT_ref, o_ref, acc_ref, m_ref, s_ref, p_ref, alpha_ref, cmax_ref, lead_ref, *,
                 tk, tv, n_chunks):
    def key_scores(c, hh):
        start = pl.multiple_of(c * tk, tk)
        ks = k_ref[0, 0, pl.ds(start, tk), :]
        return jnp.dot(ks, qT_ref[0, hh], preferred_element_type=F32)

    def pv_product(c, hh):
        pv = None
        for u in range(tk // tv):
            part = jnp.dot(vT_ref[0, 0, c * (tk // tv) + u], p_ref[hh % 2, u * tv:(u + 1) * tv, :],
                           preferred_element_type=F32)
            pv = part if pv is None else pv + part
        return pv

    def finalize():
        for hp in range(HEADS_PER_KV // 2):
            outs = []
            for hh in (2 * hp, 2 * hp + 1):
                a = acc_ref[hh]
                outs.append(a[:HEAD_DIM] / a[HEAD_DIM:HEAD_DIM + 1])
            o = jnp.concatenate(outs, axis=0).T
            o_ref[0, :, 128 * hp:128 * (hp + 1)] = o.astype(BF16)

    def spec_scores(c, hh, first):
        s = key_scores(c, hh)
        ref = s[0:1, :] if first else m_ref[hh]
        p_ref[hh % 2] = jnp.exp2((s - ref).astype(BF16))
        m_new = jnp.maximum(ref, jnp.max(s, axis=0, keepdims=True))
        alpha_ref[hh % 2] = jnp.exp2(ref - m_new)
        lead_ref[...] = jnp.maximum(lead_ref[...], m_new - ref)
        m_ref[hh] = m_new

    def spec_accumulate(c, hh):
        acc_ref[hh] = (acc_ref[hh] + pv_product(c, hh)) * alpha_ref[hh % 2]

    def spec_chunk(c, first, last):
        for hh in range(HEADS_PER_KV):
            if hh + 1 < HEADS_PER_KV:
                spec_scores(c, hh + 1, first)
            elif not last:
                spec_scores(c + 1, 0, False)
            spec_accumulate(c, hh)

    def spec_loop_body(c, carry):
        spec_chunk(c, first=False, last=False)
        return carry

    acc_ref[...] = jnp.zeros_like(acc_ref)
    lead_ref[...] = jnp.zeros_like(lead_ref)
    spec_scores(0, 0, True)
    spec_chunk(0, first=True, last=False)
    lax.fori_loop(1, n_chunks - 1, spec_loop_body, 0, unroll=CHUNK_UNROLL)
    spec_chunk(n_chunks - 1, first=False, last=True)
    finalize()

    def safe_scores(c, hh):
        s = key_scores(c, hh)
        s_ref[hh % 2] = s
        cmax_ref[hh % 2] = jnp.max(s, axis=0, keepdims=True)

    def safe_softmax(hh):
        m_old = m_ref[hh]
        m_new = jnp.maximum(m_old, cmax_ref[hh % 2])
        p_ref[hh % 2] = jnp.exp2(s_ref[hh % 2] - m_new).astype(BF16)
        alpha_ref[hh % 2] = jnp.exp2(m_old - m_new)
        m_ref[hh] = m_new

    def safe_accumulate(c, hh):
        acc_ref[hh] = acc_ref[hh] * alpha_ref[hh % 2] + pv_product(c, hh)

    def safe_chunk(c, last):
        for hh in range(HEADS_PER_KV):
            if hh + 2 < HEADS_PER_KV:
                safe_scores(c, hh + 2)
            elif not last:
                safe_scores(c + 1, hh + 2 - HEADS_PER_KV)
            if hh + 1 < HEADS_PER_KV:
                safe_softmax(hh + 1)
            elif not last:
                safe_softmax(0)
            safe_accumulate(c, hh)

    def safe_loop_body(c, carry):
        safe_chunk(c, last=False)
        return carry

    @pl.when(jnp.max(lead_ref[...]) > MAX_LEAD)
    def _exact_max_path():
        acc_ref[...] = jnp.zeros_like(acc_ref)
        m_ref[...] = jnp.full_like(m_ref, -jnp.inf)
        safe_scores(0, 0)
        safe_scores(0, 1)
        safe_softmax(0)
        lax.fori_loop(0, n_chunks - 1, safe_loop_body, 0)
        safe_chunk(n_chunks - 1, last=True)
        finalize()


def _attn_call(qT, kk, vT, tq, tk):
    B, _, _, L = qT.shape
    n_pieces, tv = vT.shape[2], vT.shape[4]
    n_chunks = L // tk
    kern = functools.partial(_attn_kernel, tk=tk, tv=tv, n_chunks=n_chunks)
    return pl.pallas_call(
        kern,
        grid=(B, N_KV_HEADS, L // tq),
        in_specs=[
            pl.BlockSpec((1, HEADS_PER_KV, HEAD_DIM, tq), lambda b, g, i: (b, g, 0, i)),
            pl.BlockSpec((1, 1, L, HEAD_DIM), lambda b, g, i: (b, g, 0, 0)),
            pl.BlockSpec((1, 1, n_pieces, V_ROWS, tv), lambda b, g, i: (b, g, 0, 0, 0)),
        ],
        out_specs=pl.BlockSpec((1, tq, HEADS_PER_KV * HEAD_DIM), lambda b, g, i: (b, i, g)),
        out_shape=jax.ShapeDtypeStruct((B, L, ATTN_WIDTH), BF16),
        scratch_shapes=[
            pltpu.VMEM((HEADS_PER_KV, V_ROWS, tq), F32),
            pltpu.VMEM((HEADS_PER_KV, 1, tq), F32),
            pltpu.VMEM((2, tk, tq), F32),
            pltpu.VMEM((2, tk, tq), BF16),
            pltpu.VMEM((2, 1, tq), F32),
            pltpu.VMEM((2, 1, tq), F32),
            pltpu.VMEM((1, tq), F32),
        ],
        compiler_params=pltpu.CompilerParams(
            dimension_semantics=("arbitrary", "arbitrary", "arbitrary"), vmem_limit_bytes=VMEM_LIMIT),
        name="attn",
    )(qT, kk, vT)


def _out_kernel(x_ref, zr_ref, att_ref, mkT_ref, mv_ref, wbr_ref, wout_ref, lnpost_ref, y_ref):
    br_pool = zr_ref[0, :, R_BR_POOL:R_BR_POOL + POOL_WIDTH]

    attn_silu = zr_ref[0, :, R_ATTN_GATE:R_ATTN_GATE + ATTN_WIDTH].astype(F32)
    br_attn = (att_ref[0].astype(F32) * attn_silu).astype(BF16)

    cross_parts = []
    for hh in range(N_X_HEADS):
        lo_c = hh * X_HEAD_DIM
        xq = zr_ref[0, :, R_XQ + lo_c:R_XQ + lo_c + X_HEAD_DIM]
        s = jnp.dot(xq, mkT_ref[0, lo_c:lo_c + X_HEAD_DIM, :], preferred_element_type=F32)
        p = jnp.exp2(s - jnp.max(s, axis=-1, keepdims=True))
        den = jnp.sum(p, axis=-1, keepdims=True)
        o = jnp.dot(p.astype(BF16), mv_ref[0, :, lo_c:lo_c + X_HEAD_DIM], preferred_element_type=F32)
        cross_parts.append(o / den)
    x_silu = zr_ref[0, :, R_X_GATE:R_X_GATE + X_WIDTH].astype(F32)
    br_cross = (jnp.concatenate(cross_parts, axis=1) * x_silu).astype(BF16)

    merged2 = None
    for n, br in enumerate((br_pool, br_attn, br_cross)):
        proj = jnp.dot(br, wbr_ref[n], preferred_element_type=F32)
        gate_tanh = zr_ref[0, :, R_MERGE + n * D_MODEL:R_MERGE + (n + 1) * D_MODEL].astype(F32)
        gated2 = proj + gate_tanh * proj
        merged2 = gated2 if merged2 is None else merged2 + gated2

    o = jnp.dot(merged2.astype(BF16), wout_ref[...], preferred_element_type=F32)
    ms = jnp.mean(o * o, axis=-1, keepdims=True)
    y_ref[0] = x_ref[0] + o * lax.rsqrt(ms + EPS) * lnpost_ref[...]


def _out_call(x, zr, att, mkT, mv, w_branch, w_out, ln_post, tm):
    B, L, _ = x.shape
    M = mv.shape[1]
    return pl.pallas_call(
        _out_kernel,
        grid=(B, L // tm),
        in_specs=[
            pl.BlockSpec((1, tm, D_MODEL), lambda b, i: (b, i, 0)),
            pl.BlockSpec((1, tm, REST_WIDTH), lambda b, i: (b, i, 0)),
            pl.BlockSpec((1, tm, ATTN_WIDTH), lambda b, i: (b, i, 0)),
            pl.BlockSpec((1, X_WIDTH, M), lambda b, i: (b, 0, 0)),
            pl.BlockSpec((1, M, X_WIDTH), lambda b, i: (b, 0, 0)),
            _const_spec((N_BRANCH, BRANCH_WIDTH, D_MODEL)),
            _const_spec((D_MODEL, D_MODEL)),
            _const_spec((1, D_MODEL)),
        ],
        out_specs=pl.BlockSpec((1, tm, D_MODEL), lambda b, i: (b, i, 0)),
        out_shape=jax.ShapeDtypeStruct((B, L, D_MODEL), F32),
        compiler_params=pltpu.CompilerParams(
            dimension_semantics=("arbitrary", "arbitrary"), vmem_limit_bytes=VMEM_LIMIT),
        name="mixout",
    )(x, zr, att, mkT, mv, w_branch, w_out, ln_post)


def _rope_tables(L, q_norm):
    pos = jnp.arange(L, dtype=jnp.int32)
    row = (pos // GRID_W).astype(F32)
    col = (pos % GRID_W).astype(F32)
    axis_dim = HEAD_DIM // 2
    inv = ROPE_THETA ** (-jnp.arange(0, axis_dim, 2, dtype=F32) / axis_dim)
    ang = jnp.concatenate([row[:, None] * inv, col[:, None] * inv], axis=-1)
    cos = jnp.repeat(jnp.cos(ang), 2, axis=-1)
    sin = jnp.repeat(jnp.sin(ang), 2, axis=-1) * jnp.tile(jnp.array([-1.0, 1.0], F32), HEAD_DIM // 2)
    g = q_norm.astype(F32) * (LOG2_E * HEAD_DIM ** -0.5)
    g_swapped = g.reshape(HEAD_DIM // 2, 2)[:, ::-1].reshape(HEAD_DIM)
    cos_qT = jnp.tile((cos * g).T, (2, 1))
    sin_qT = jnp.tile((sin * g_swapped).T, (2, 1))
    return jnp.tile(cos, (1, 2)), jnp.tile(sin, (1, 2)), cos_qT, sin_qT


def _layer(x, mem, prm, tables, tm, tq, tk):
    cos_t, sin_t, cos_qT, sin_qT = tables
    qT, kk, vT, zr = _proj_call(x, prm["ln_pre"], prm["w_qkv"], prm["w_pg"], prm["w_rest"], prm["bd"],
                                prm["kn"], cos_t, sin_t, cos_qT, sin_qT, prm["b_merge_row"], prm["w_pool"],
                                prm["pool_scale"], tm, min(tk, tm))
    mkT, mv = _mem_call(mem, prm["ln_mem"], prm["w_mem_kv"])
    att = _attn_call(qT, kk, vT, tq, tk)
    return _out_call(x, zr, att, mkT, mv, prm["w_branch"], prm["w_out"], prm["ln_post"], min(2 * tm, x.shape[1]))


def _prep_params(ln_pre, ln_post, ln_mem, w_in, b_merge, q_norm, k_norm, w_pool, pool_scale, w_mem_kv,
                 w_branch, w_out):
    head_id = np.arange(KV_WIDTH) // HEAD_DIM
    bd = (head_id[:, None] == head_id[None, :]).astype(np.float32) / HEAD_DIM
    pg_scale = np.ones((PG_WIDTH,), np.float32)
    pg_scale[POOL_WIDTH:] = 0.5
    rest_scale = np.full((W_REST_WIDTH,), 0.5, np.float32)
    rest_scale[R_XQ - POOL_WIDTH:R_XQ - POOL_WIDTH + X_WIDTH] = LOG2_E * X_HEAD_DIM ** -0.5
    return {
        "ln_pre": ln_pre.reshape(1, D_MODEL),
        "ln_post": ln_post.reshape(1, D_MODEL),
        "ln_mem": ln_mem.reshape(1, D_MODEL),
        "w_qkv": w_in[:, _Q0:_AG0].astype(BF16),
        "w_pg": (w_in[:, :_Q0] * pg_scale).astype(BF16),
        "w_rest": (w_in[:, _AG0:] * rest_scale).astype(BF16),
        "bd": jnp.asarray(bd, BF16),
        "kn": jnp.tile(k_norm, N_KV_HEADS).reshape(1, KV_WIDTH),
        "w_pool": w_pool.astype(BF16),
        "pool_scale": pool_scale.reshape(1, POOL_WIDTH),
        "w_mem_kv": w_mem_kv.astype(BF16),
        "w_branch": w_branch.astype(BF16),
        "b_merge_row": b_merge.reshape(1, N_BRANCH * D_MODEL),
        "w_out": (0.5 * w_out).astype(BF16),
    }


def kernel(x_prompt, x_sample, mem_prompt, mem_sample, ln_pre, ln_post, ln_mem, w_in, b_merge, q_norm, k_norm,
           w_pool, pool_scale, w_mem_kv, w_branch, w_out):
    prm = _prep_params(ln_pre, ln_post, ln_mem, w_in, b_merge, q_norm, k_norm, w_pool, pool_scale, w_mem_kv,
                       w_branch, w_out)
    tables = _rope_tables(max(x_prompt.shape[1], x_sample.shape[1]), q_norm)
    outs = []
    for x, mem in ((x_prompt, mem_prompt), (x_sample, mem_sample)):
        L = x.shape[1]
        tm = min(512, L)
        tk = max(tm, L // ATTN_CHUNKS)
        outs.append(_layer(x, mem, prm, tables, tm=tm, tq=min(512, L), tk=tk))
    return tuple(outs)
```
